```python
import math
import jax, jax.numpy as jnp
from jax import lax
import numpy as np

D_MODEL = 1024
BATCH = 4
SEQ = 4096
DEPTH = 1
DEC_BATCH = 32
DEC_SEQ = 8
PAST_LEN = 8192
PAGE_SIZE = 128

D_MIX = 2 * D_MODEL
HEAD_DIM = 64
D_ATT = D_MIX // 2
N_ATT_HEADS = D_ATT // HEAD_DIM
DILATED_BRANCHES = ((128, 1), (512, 4), (2048, 16))
MAX_WINDOW = 2048
Q_BLOCK = 128
D_SSM = D_MIX - D_ATT
N_SSM_HEADS = D_SSM // HEAD_DIM
SSM_STATE = 128
N_SSM_GROUPS = 4
CONV_WIDTH = 4
SSD_CHUNK = 128
D_CONV = D_SSM + 2 * N_SSM_GROUPS * SSM_STATE
D_FF = ((-(-8 * D_MODEL // 3) + 255) // 256) * 256
D_IN_PROJ = 3 * D_ATT + D_SSM + D_CONV + N_SSM_HEADS
DEEPNORM_ALPHA = (2.0 * DEPTH) ** 0.25
DEEPNORM_BETA = (8.0 * DEPTH) ** -0.25
LN_EPS = 1e-5
RMS_EPS = 1e-5

kernel_name = "hymba_longnet_ssd_deepnorm_step"


def _alibi_slopes():
    n = N_ATT_HEADS
    return jnp.asarray(2.0 ** (-8.0 * np.arange(1, n + 1) / n), dtype=jnp.float32)


def _layer_norm(x, g, b):
    xf = x.astype(jnp.float32)
    mu = jnp.mean(xf, -1, keepdims=True)
    var = jnp.mean(jnp.square(xf - mu), -1, keepdims=True)
    y = (xf - mu) * lax.rsqrt(var + LN_EPS) * g.astype(jnp.float32) + b.astype(jnp.float32)
    return y.astype(x.dtype)


def _rms_norm(xf, g):
    return xf * lax.rsqrt(jnp.mean(jnp.square(xf), -1, keepdims=True) + RMS_EPS) * g.astype(jnp.float32)


def _dilated_block(q, qpos, k, v, slopes):
    qf = q.astype(jnp.float32) * (HEAD_DIM ** -0.5)
    outs, lses = [], []
    for window, dil in DILATED_BRANCHES:
        dist = jnp.arange(window // dil + 1, dtype=jnp.int32) * dil
        idx = qpos[:, None] - dist[None, :]
        valid = idx >= 0
        idx = jnp.clip(idx, 0, k.shape[1] - 1)
        kg = jnp.take(k, idx, axis=1).astype(jnp.float32)
        vg = jnp.take(v, idx, axis=1).astype(jnp.float32)
        s = jnp.einsum('bqhd,bqjhd->bhqj', qf, kg)
        s = s - slopes[:, None, None] * dist.astype(jnp.float32)[None, None, :]
        s = jnp.where(valid[None, None], s, -jnp.inf)
        m = jnp.max(s, -1, keepdims=True)
        p = jnp.exp(s - m)
        den = jnp.sum(p, -1, keepdims=True)
        outs.append(jnp.einsum('bhqj,bqjhd->bqhd', p / den, vg))
        lses.append((m + jnp.log(den))[..., 0])
    w = jax.nn.softmax(jnp.stack(lses, 0), axis=0)
    w = jnp.transpose(w, (0, 1, 3, 2))[..., None]
    return jnp.sum(w * jnp.stack(outs, 0), axis=0)


def _dilated_attention(q, k, v, q_start, slopes):
    b, t, h, dh = q.shape
    if t % Q_BLOCK == 0 and t > Q_BLOCK:
        nb = t // Q_BLOCK
        qb = jnp.moveaxis(q.reshape(b, nb, Q_BLOCK, h, dh), 1, 0)

        def one(args):
            q_blk, i = args
            qpos = q_start + i * Q_BLOCK + jnp.arange(Q_BLOCK, dtype=jnp.int32)
            return _dilated_block(q_blk, qpos, k, v, slopes)

        out = lax.map(one, (qb, jnp.arange(nb, dtype=jnp.int32)))
        return jnp.moveaxis(out, 0, 1).reshape(b, t, h, dh)
    qpos = q_start + jnp.arange(t, dtype=jnp.int32)
    return _dilated_block(q, qpos, k, v, slopes)


def _ssd_chunked(x, dt, a, b_in, c_in, h0):
    bsz, t, h, p = x.shape
    g, n = b_in.shape[2], b_in.shape[3]
    r = h // g
    L = SSD_CHUNK if t % SSD_CHUNK == 0 else t
    nc = t // L
    xdt = (x * dt[..., None]).reshape(bsz, nc, L, g, r, p)
    la = (dt * a).reshape(bsz, nc, L, g, r)
    bc = b_in.reshape(bsz, nc, L, g, n)
    cc = c_in.reshape(bsz, nc, L, g, n)
    cs = jnp.cumsum(la, axis=2)
    seg = cs[:, :, :, None] - cs[:, :, None, :]
    causal = jnp.tril(jnp.ones((L, L), dtype=bool))[None, None, :, :, None, None]
    decay = jnp.exp(jnp.where(causal, seg, -jnp.inf))
    cb = jnp.einsum('bclgn,bcsgn->bclsg', cc, bc)
    y_diag = jnp.einsum('bclsg,bclsgr,bcsgrp->bclgrp', cb, decay, xdt)
    to_end = jnp.exp(cs[:, :, -1:] - cs)
    chunk_states = jnp.einsum('bclgn,bclgr,bclgrp->bcgrpn', bc, to_end, xdt)
    chunk_decay = jnp.exp(cs[:, :, -1])

    def step(hc, inp):
        st, dec = inp
        return hc * dec[..., None, None] + st, hc

    h_fin, h_prev = lax.scan(step, h0.reshape(bsz, g, r, p, n),
                             (jnp.moveaxis(chunk_states, 1, 0), jnp.moveaxis(chunk_decay, 1, 0)))
    y_off = jnp.einsum('bclgn,cbgrpn,bclgr->bclgrp', cc, h_prev, jnp.exp(cs))
    y = (y_diag + y_off).reshape(bsz, t, h, p)
    return y, h_fin.reshape(bsz, h, p, n)


def _hybrid_layer(x, k_past, v_past, conv_prefix, h0, w_in, conv_w, conv_b, dt_bias, a_log,
                  d_skip, attn_norm_g, ssm_norm_g, w_out, ln1_g, ln1_b, w_gate, w_up, w_down,
                  ln2_g, ln2_b):
    f32 = jnp.float32
    bsz, t, _ = x.shape
    proj = x @ w_in
    cuts = np.cumsum([D_ATT, D_ATT, D_ATT, D_SSM, D_CONV])
    q, k, v, z, xbc_raw, dt_raw = jnp.split(proj, cuts, axis=-1)
    q = q.reshape(bsz, t, N_ATT_HEADS, HEAD_DIM)
    k = k.reshape(bsz, t, N_ATT_HEADS, HEAD_DIM)
    v = v.reshape(bsz, t, N_ATT_HEADS, HEAD_DIM)

    k_all = jnp.concatenate([k_past.astype(k.dtype), k], axis=1)
    v_all = jnp.concatenate([v_past.astype(v.dtype), v], axis=1)
    attn = _dilated_attention(q, k_all, v_all, k_past.shape[1], _alibi_slopes())
    attn = _rms_norm(attn.reshape(bsz, t, D_ATT), attn_norm_g)

    xpad = jnp.concatenate([conv_prefix.astype(f32), xbc_raw.astype(f32)], axis=1)
    conv_new = xpad[:, -(CONV_WIDTH - 1):]
    xbc = lax.conv_general_dilated(xpad, conv_w.astype(f32)[:, None, :], window_strides=(1,),
                                   padding='VALID', dimension_numbers=('NWC', 'WIO', 'NWC'),
                                   feature_group_count=D_CONV) + conv_b.astype(f32)
    xbc = jax.nn.silu(xbc)
    xs, bs, cs_ = jnp.split(xbc, [D_SSM, D_SSM + N_SSM_GROUPS * SSM_STATE], axis=-1)
    xs = xs.reshape(bsz, t, N_SSM_HEADS, HEAD_DIM)
    bs = bs.reshape(bsz, t, N_SSM_GROUPS, SSM_STATE)
    cs_ = cs_.reshape(bsz, t, N_SSM_GROUPS, SSM_STATE)
    dt = jax.nn.softplus(dt_raw.astype(f32) + dt_bias.astype(f32))
    a = -jnp.exp(a_log.astype(f32))
    y_ssm, h_fin = _ssd_chunked(xs, dt, a, bs, cs_, h0.astype(f32))
    y_ssm = (y_ssm + d_skip.astype(f32)[:, None] * xs).reshape(bsz, t, D_SSM)
    y_ssm = _rms_norm(y_ssm * jax.nn.silu(z.astype(f32)), ssm_norm_g)

    mix = jnp.concatenate([attn, y_ssm], axis=-1).astype(x.dtype) @ w_out
    hdn = _layer_norm(DEEPNORM_ALPHA * x + mix, ln1_g, ln1_b)
    ffn = (jax.nn.silu(hdn @ w_gate) * (hdn @ w_up)) @ w_down
    y = _layer_norm(DEEPNORM_ALPHA * hdn + ffn, ln2_g, ln2_b)
    return y, k, v, h_fin.astype(x.dtype), conv_new.astype(x.dtype)


def setup_inputs(seed: int = 0) -> dict:
    key = jax.random.key(seed)
    ks = jax.random.split(key, 24)
    nrm = jax.random.normal
    wb = min(MAX_WINDOW, PAST_LEN)
    dt0 = jnp.exp(jax.random.uniform(ks[10], (DEPTH, N_SSM_HEADS), minval=math.log(1e-3), maxval=math.log(1e-1)))
    return {
        "x_prompt": nrm(ks[0], (BATCH, SEQ, D_MODEL), jnp.float32),
        "x_sample": nrm(ks[1], (DEC_BATCH, DEC_SEQ, D_MODEL), jnp.float32),
        "cache_k_win": nrm(ks[2], (DEPTH, DEC_BATCH, wb, N_ATT_HEADS, HEAD_DIM), jnp.float32),
        "cache_v_win": nrm(ks[3], (DEPTH, DEC_BATCH, wb, N_ATT_HEADS, HEAD_DIM), jnp.float32),
        "state_ssm": 0.1 * nrm(ks[4], (DEPTH, DEC_BATCH, N_SSM_HEADS, HEAD_DIM, SSM_STATE), jnp.float32),
        "state_conv": nrm(ks[5], (DEPTH, DEC_BATCH, CONV_WIDTH - 1, D_CONV), jnp.float32),
        "w_in": nrm(ks[6], (DEPTH, D_MODEL, D_IN_PROJ), jnp.float32) * D_MODEL ** -0.5,
        "conv_w": nrm(ks[7], (DEPTH, CONV_WIDTH, D_CONV), jnp.float32) * CONV_WIDTH ** -0.5,
        "conv_b": 0.01 * nrm(ks[8], (DEPTH, D_CONV), jnp.float32),
        "dt_bias": dt0 + jnp.log(-jnp.expm1(-dt0)),
        "a_log": jnp.log(jax.random.uniform(ks[11], (DEPTH, N_SSM_HEADS), minval=1.0, maxval=16.0)),
        "d_skip": 1.0 + 0.1 * nrm(ks[12], (DEPTH, N_SSM_HEADS), jnp.float32),
        "attn_norm_g": 1.0 + 0.05 * nrm(ks[13], (DEPTH, D_ATT), jnp.float32),
        "ssm_norm_g": 1.0 + 0.05 * nrm(ks[14], (DEPTH, D_SSM), jnp.float32),
        "w_out": nrm(ks[15], (DEPTH, D_MIX, D_MODEL), jnp.float32) * (D_MIX ** -0.5 * DEEPNORM_BETA),
        "ln1_g": 1.0 + 0.05 * nrm(ks[16], (DEPTH, D_MODEL), jnp.float32),
        "ln1_b": 0.01 * nrm(ks[17], (DEPTH, D_MODEL), jnp.float32),
        "w_gate": nrm(ks[18], (DEPTH, D_MODEL, D_FF), jnp.float32) * D_MODEL ** -0.5,
        "w_up": nrm(ks[19], (DEPTH, D_MODEL, D_FF), jnp.float32) * D_MODEL ** -0.5,
        "w_down": nrm(ks[20], (DEPTH, D_FF, D_MODEL), jnp.float32) * (D_FF ** -0.5 * DEEPNORM_BETA),
        "ln2_g": 1.0 + 0.05 * nrm(ks[21], (DEPTH, D_MODEL), jnp.float32),
        "ln2_b": 0.01 * nrm(ks[22], (DEPTH, D_MODEL), jnp.float32),
    }


def reference(x_prompt, x_sample, cache_k_win, cache_v_win, state_ssm, state_conv, w_in, conv_w,
              conv_b, dt_bias, a_log, d_skip, attn_norm_g, ssm_norm_g, w_out, ln1_g, ln1_b,
              w_gate, w_up, w_down, ln2_g, ln2_b):
    bp, tp, _ = x_prompt.shape
    keep = min(MAX_WINDOW, tp)
    hp, hs = x_prompt, x_sample
    kp_l, vp_l, sp_l, cp_l, ks_l, vs_l, ss_l, cs_l = [], [], [], [], [], [], [], []
    for l in range(DEPTH):
        w = (w_in[l], conv_w[l], conv_b[l], dt_bias[l], a_log[l], d_skip[l], attn_norm_g[l],
             ssm_norm_g[l], w_out[l], ln1_g[l], ln1_b[l], w_gate[l], w_up[l], w_down[l],
             ln2_g[l], ln2_b[l])
        k0 = jnp.zeros((bp, 0, N_ATT_HEADS, HEAD_DIM), hp.dtype)
        c0 = jnp.zeros((bp, CONV_WIDTH - 1, D_CONV), hp.dtype)
        s0 = jnp.zeros((bp, N_SSM_HEADS, HEAD_DIM, SSM_STATE), hp.dtype)
        hp, kp, vp, sp, cp = _hybrid_layer(hp, k0, k0, c0, s0, *w)
        kp_l.append(kp[:, tp - keep:]); vp_l.append(vp[:, tp - keep:]); sp_l.append(sp); cp_l.append(cp)
        hs, kn, vn, sn, cn = _hybrid_layer(hs, cache_k_win[l], cache_v_win[l], state_conv[l],
                                           state_ssm[l], *w)
        ks_l.append(kn); vs_l.append(vn); ss_l.append(sn); cs_l.append(cn)
    return (hp, hs, jnp.stack(kp_l), jnp.stack(vp_l), jnp.stack(sp_l), jnp.stack(cp_l),
            jnp.stack(ks_l), jnp.stack(vs_l), jnp.stack(ss_l), jnp.stack(cs_l))
```

```python
import functools
import math

import numpy as np
import jax
import jax.numpy as jnp
from jax import lax
from jax.experimental import pallas as pl
from jax.experimental.pallas import tpu as pltpu

F32 = jnp.float32
BF16 = jnp.bfloat16

D_MODEL = 1024
HEAD_DIM = 64
N_HEADS = 16
N_PAIRS = N_HEADS // 2
D_ATT = N_HEADS * HEAD_DIM
D_SSM = N_HEADS * HEAD_DIM
SSM_STATE = 128
N_GROUPS = 4
HEADS_PER_GROUP = N_HEADS // N_GROUPS
GROUP_W = HEADS_PER_GROUP * HEAD_DIM
CONV_WIDTH = 4
D_CONV = D_SSM + 2 * N_GROUPS * SSM_STATE
D_FF = 2816
FF_CHUNK = D_FF // 2
CHUNK = 128
DILATED_BRANCHES = ((128, 1), (512, 4), (2048, 16))
WIN = 128
MAX_WINDOW = 2048
LN_EPS = 1e-5
RMS_EPS = 1e-5
LANES = 128
DT_PAD = LANES
D_PROJ_PAD = 3 * D_ATT + D_SSM + D_CONV + DT_PAD
LOG2E = math.log2(math.e)
Q_SCALE = HEAD_DIM ** -0.5 * LOG2E
NEG = -1e30
VMEM_LIMIT = 56 * 1024 * 1024


def _alibi_slopes():
    return 2.0 ** (-8.0 * np.arange(1, N_HEADS + 1) / N_HEADS)


def _params(sem, vmem=None):
    return pltpu.CompilerParams(dimension_semantics=sem, vmem_limit_bytes=vmem)


def _const_spec(shape):
    nd = len(shape)
    return pl.BlockSpec(shape, lambda *_: (0,) * nd)


def _split_bf16(x):
    hi = x.astype(BF16)
    lo = (x - hi.astype(F32)).astype(BF16)
    return hi, lo


def _dot(a, b):
    return jnp.dot(a, b, preferred_element_type=F32)


def _dot_nt(a, b):
    return lax.dot_general(a, b, (((1,), (1,)), ((), ())), preferred_element_type=F32)


def _dot_split(x, e):
    hi, lo = _split_bf16(x)
    return _dot(hi, e) + _dot(lo, e)


def _layer_norm(x, g, b):
    mu = jnp.mean(x, axis=-1, keepdims=True)
    xc = x - mu
    var = jnp.mean(xc * xc, axis=-1, keepdims=True)
    return xc * lax.rsqrt(var + LN_EPS) * g + b


def _rms_norm(x, g):
    return x * lax.rsqrt(jnp.mean(x * x, axis=-1, keepdims=True) + RMS_EPS) * g


def _silu(x):
    return x * (1.0 / (1.0 + jnp.exp(-x)))


def _in_proj_kernel(x_ref, w_ref, q_ref, kb_ref, vb_ref, kf_ref, vf_ref, z_ref, xbc_ref, dt_ref):
    xb = x_ref[...].astype(BF16)

    def seg(lo, hi):
        return _dot(xb, w_ref[:, lo:hi])

    q_ref[...] = (seg(0, D_ATT) * Q_SCALE).astype(BF16)
    k = seg(D_ATT, 2 * D_ATT)
    kb_ref[...] = k.astype(BF16)
    kf_ref[...] = k
    v = seg(2 * D_ATT, 3 * D_ATT)
    vb_ref[...] = v.astype(BF16)
    vf_ref[...] = v
    o = 3 * D_ATT
    z_ref[...] = seg(o, o + D_SSM).astype(BF16)
    o += D_SSM
    xbc_ref[...] = seg(o, o + D_CONV)
    o += D_CONV
    dt_ref[...] = seg(o, o + DT_PAD)


def _in_proj(x, w_pad, tm, keep):
    b, t, _ = x.shape
    nt = t // tm
    skip = (t - keep) // tm
    row = lambda bi, i: (bi, i, 0)
    kept = lambda bi, i: (bi, jnp.maximum(i - skip, 0), 0)

    def spec(w, imap=row):
        return pl.BlockSpec((None, tm, w), imap)

    sds = jax.ShapeDtypeStruct
    return pl.pallas_call(
        _in_proj_kernel,
        grid=(b, nt),
        in_specs=[spec(D_MODEL),
                  pl.BlockSpec((D_MODEL, D_PROJ_PAD), lambda bi, i: (0, 0), pipeline_mode=pl.Buffered(1))],
        out_specs=[spec(D_ATT), spec(D_ATT), spec(D_ATT), spec(D_ATT, kept), spec(D_ATT, kept),
                   spec(D_SSM), spec(D_CONV), spec(DT_PAD)],
        out_shape=[sds((b, t, D_ATT), BF16), sds((b, t, D_ATT), BF16), sds((b, t, D_ATT), BF16),
                   sds((b, keep, D_ATT), F32), sds((b, keep, D_ATT), F32),
                   sds((b, t, D_SSM), BF16), sds((b, t, D_CONV), F32), sds((b, t, DT_PAD), F32)],
        compiler_params=_params(("arbitrary", "arbitrary"), VMEM_LIMIT),
        name="in_proj",
    )(x, w_pad)


def _branch_bias(dil):
    r = np.arange(WIN)[:, None]
    c = np.arange(2 * WIN)[None, :]
    ju = WIN + r - c
    valid = (ju >= 0) & (ju <= WIN)
    slopes = _alibi_slopes()[:, None, None]
    base = -slopes * (ju * dil)[None] * LOG2E
    full = np.where(valid[None], base, NEG)
    first = np.where((valid & (c >= WIN))[None], base, NEG)
    return np.stack([full, first]).astype(np.float32)


def _attn_branch_kernel(q_ref, kc_ref, kh_ref, vc_ref, vh_ref, bias_ref, o_ref, st_ref, *, tq):
    first_tile = pl.program_id(2) == 0
    lane = lax.broadcasted_iota(jnp.int32, (1, LANES), 1)
    head0 = lane < HEAD_DIM
    keep0 = jnp.where(head0, 1.0, 0.0).astype(BF16)
    keep1 = jnp.where(head0, 0.0, 1.0).astype(BF16)
    ones_blk = jnp.ones((tq + WIN, LANES), BF16)
    for hp in range(N_PAIRS):
        sl = slice(hp * LANES, (hp + 1) * LANES)
        kfull = jnp.concatenate([kh_ref[:, sl], kc_ref[:, sl]], axis=0)
        vaug = jnp.concatenate(
            [jnp.concatenate([vh_ref[:, sl], vc_ref[:, sl]], axis=0), ones_blk], axis=1)
        stat_lane = (lane == hp) | (lane == HEAD_DIM + hp)
        for qb in range(tq // WIN):
            rows = slice(qb * WIN, (qb + 1) * WIN)
            qblk = q_ref[rows, sl]
            q2 = jnp.concatenate([qblk * keep0, qblk * keep1], axis=0)
            kblk = kfull[qb * WIN:qb * WIN + 2 * WIN]
            s = _dot_nt(q2, kblk)
            if qb == 0:
                var = jnp.where(first_tile, 1, 0)
                bias = jnp.concatenate([bias_ref[var, 2 * hp], bias_ref[var, 2 * hp + 1]], axis=0)
            else:
                bias = jnp.concatenate([bias_ref[0, 2 * hp], bias_ref[0, 2 * hp + 1]], axis=0)
            s = s + bias
            m = jnp.max(s, axis=1, keepdims=True)
            p = jnp.exp2(s - m).astype(BF16)
            vblk = vaug[qb * WIN:qb * WIN + 2 * WIN]
            r0 = _dot(p[:WIN], vblk)
            r1 = _dot(p[WIN:], vblk)
            den = jnp.where(head0, r0[:, LANES:], r1[:, LANES:])
            num = jnp.where(head0, r0[:, :LANES], r1[:, :LANES])
            o_ref[rows, sl] = (num / den).astype(BF16)
            lse = jnp.where(head0, m[:WIN], m[WIN:]) + jnp.log2(den)
            prev = jnp.zeros((WIN, LANES), F32) if hp == 0 else st_ref[rows, :]
            st_ref[rows, :] = jnp.where(stat_lane, lse, prev)


def _attn_branch(q, k, v, dil):
    b, t, _ = q.shape
    tc = t // dil
    tq = min(512, tc)
    view = lambda a: a.reshape(b, tc, dil * a.shape[-1])
    cur = pl.BlockSpec((None, tq, D_ATT), lambda bi, r, i: (bi, i, r))
    halo = pl.BlockSpec((None, WIN, D_ATT),
                        lambda bi, r, i: (bi, jnp.maximum(i * (tq // WIN) - 1, 0), r))
    bias = jnp.asarray(_branch_bias(dil))
    out, st = pl.pallas_call(
        functools.partial(_attn_branch_kernel, tq=tq),
        grid=(b, dil, tc // tq),
        in_specs=[cur, cur, halo, cur, halo, _const_spec(bias.shape)],
        out_specs=[cur, pl.BlockSpec((None, tq, LANES), lambda bi, r, i: (bi, i, r))],
        out_shape=[jax.ShapeDtypeStruct((b, tc, dil * D_ATT), BF16),
                   jax.ShapeDtypeStruct((b, tc, dil * LANES), F32)],
        compiler_params=_params(("arbitrary",) * 3, VMEM_LIMIT),
        name=f"attn_dil{dil}",
    )(view(q), view(k), view(k), view(v), view(v), bias)
    return out.reshape(b, t, D_ATT), st.reshape(b, t, LANES)


def _sample_table(wb, tn, kpad):
    i = np.arange(tn)[:, None]
    c = np.arange(wb + kpad)[None, :]
    dist = wb + i - c
    mult = np.zeros(dist.shape)
    for window, dil in DILATED_BRANCHES:
        mult += (dist >= 0) & (dist <= window) & (dist % dil == 0) & (c < wb + tn)
    with np.errstate(divide="ignore"):
        logm = np.where(mult > 0, np.log2(np.maximum(mult, 1)), NEG)
    slopes = _alibi_slopes()[:, None, None]
    tab = np.where(mult[None] > 0, -slopes * dist[None] * LOG2E + logm[None], NEG)
    return tab.reshape(N_HEADS * tn, wb + kpad).astype(np.float32)


def _sample_attn_kernel(q_ref, kn_ref, vn_ref, ck_ref, cv_ref, tab_ref, o_ref, *, tn, kpad):
    rows = N_HEADS * tn
    row_head = lax.shift_right_logical(lax.broadcasted_iota(jnp.int32, (rows, D_ATT), 0), int(math.log2(tn)))
    lane_head = lax.shift_right_logical(lax.broadcasted_iota(jnp.int32, (rows, D_ATT), 1), int(math.log2(HEAD_DIM)))
    diag = row_head == lane_head
    qt = jnp.concatenate([q_ref[...].astype(F32)] * N_HEADS, axis=0)
    qbd = jnp.where(diag, qt, 0.0).astype(BF16)
    pad = jnp.zeros((kpad - tn, D_ATT), F32)
    k_all = jnp.concatenate([ck_ref[...].astype(BF16),
                             jnp.concatenate([kn_ref[...], pad], axis=0).astype(BF16)], axis=0)
    v_all = jnp.concatenate([cv_ref[...].astype(BF16),
                             jnp.concatenate([vn_ref[...], pad], axis=0).astype(BF16)], axis=0)
    s = _dot_nt(qbd, k_all) + tab_ref[...]
    m = jnp.max(s, axis=1, keepdims=True)
    p = jnp.exp2(s - m).astype(BF16)
    den = jnp.sum(p.astype(F32), axis=1, keepdims=True)
    full = _dot(p, v_all) / den
    full = jnp.where(diag, full, 0.0)
    acc = full[0:tn]
    for h in range(1, N_HEADS):
        acc = acc + full[h * tn:(h + 1) * tn]
    o_ref[...] = acc


def _sample_attn(q, k_new, v_new, cache_k, cache_v):
    b, tn, _ = q.shape
    wb = cache_k.shape[1]
    kpad = LANES
    tab = jnp.asarray(_sample_table(wb, tn, kpad))
    new = pl.BlockSpec((None, tn, D_ATT), lambda bi: (bi, 0, 0))
    cache = pl.BlockSpec((None, wb, D_ATT), lambda bi: (bi, 0, 0))
    return pl.pallas_call(
        functools.partial(_sample_attn_kernel, tn=tn, kpad=kpad),
        grid=(b,),
        in_specs=[new, new, new, cache, cache, _const_spec(tab.shape)],
        out_specs=new,
        out_shape=jax.ShapeDtypeStruct((b, tn, D_ATT), F32),
        compiler_params=_params(("arbitrary",), VMEM_LIMIT),
        name="sample_attn",
    )(q, k_new, v_new, cache_k, cache_v, tab)


def _expand_matrix(width):
    e = np.zeros((LANES, N_HEADS * width), np.float32)
    for h in range(N_HEADS):
        e[h, h * width:(h + 1) * width] = 1.0
    return e


def _ssd_kernel(xbc_ref, dt_ref, z_ref, h0_ref, pre_ref, convw_ref, convb_ref, dtb_ref, a_ref, dskip_ref,
                gn_ref, tri_ref, e128_ref, e64_ref, y_ref, hout_ref, xpad_sc, h_sc, *, l_in, n_chunks):
    c = pl.program_id(1)
    pre_rows = 8

    @pl.when(c == 0)
    def _():
        h_sc[...] = h0_ref[...]
        xpad_sc[0:pre_rows, :] = pre_ref[...]
        if l_in < CHUNK:
            xpad_sc[pre_rows + l_in:, :] = jnp.zeros((CHUNK - l_in, D_CONV), F32)

    xpad_sc[pre_rows:pre_rows + l_in, :] = xbc_ref[...]
    acc = convb_ref[...]
    for w in range(CONV_WIDTH):
        o = pre_rows - (CONV_WIDTH - 1) + w
        acc = acc + convw_ref[w:w + 1, :] * xpad_sc[o:o + CHUNK, :]
    xbc = _silu(acc)
    if n_chunks > 1:
        xpad_sc[0:pre_rows, :] = xpad_sc[CHUNK:CHUNK + pre_rows, :]

    xs = xbc[:, :D_SSM]
    bmat = xbc[:, D_SSM:D_SSM + N_GROUPS * SSM_STATE].astype(BF16)
    cmat = xbc[:, D_SSM + N_GROUPS * SSM_STATE:].astype(BF16)

    draw = dt_ref[...] + dtb_ref[...]
    dt = jnp.maximum(draw, 0.0) + jnp.log1p(jnp.exp(-jnp.abs(draw)))
    if l_in < CHUNK:
        dt = jnp.concatenate([dt, jnp.zeros((CHUNK - l_in, DT_PAD), F32)], axis=0)
    la = dt * a_ref[...]
    tri = tri_ref[...]
    la_hi, la_lo = _split_bf16(la)
    cs = _dot(tri, la_hi) + _dot(tri, la_lo)
    cs_t = cs.T
    cs128 = _dot_split(cs, e128_ref[...])
    cs64 = _dot_split(cs, e64_ref[...])
    dt64 = _dot_split(dt, e64_ref[...])
    xdt = xs * dt64
    decay_in = jnp.exp(cs64)
    cs_end = cs64[CHUNK - 1:CHUNK, :]
    xw = xdt * jnp.exp(cs_end - cs64)

    lane = lax.broadcasted_iota(jnp.int32, (1, LANES), 1)
    head0 = lane < HEAD_DIM
    causal = (lax.broadcasted_iota(jnp.int32, (CHUNK, CHUNK), 0)
              >= lax.broadcasted_iota(jnp.int32, (CHUNK, CHUNK), 1))

    y_parts = []
    for g in range(N_GROUPS):
        gs = slice(g * SSM_STATE, (g + 1) * SSM_STATE)
        xg = slice(g * GROUP_W, (g + 1) * GROUP_W)
        b_g = bmat[:, gs]
        c_g = cmat[:, gs]
        cb = _dot_nt(c_g, b_g)
        h_prev = h_sc[xg, :]
        y_off = _dot_nt(c_g, h_prev.astype(BF16)) * decay_in[:, xg]
        y_diag = []
        for hp in range(HEADS_PER_GROUP // 2):
            gmats = []
            for j in range(2):
                h = g * HEADS_PER_GROUP + 2 * hp + j
                seg = cs128[:, h * LANES:(h + 1) * LANES] - cs_t[h:h + 1, :]
                gmats.append((cb * jnp.exp(jnp.where(causal, seg, NEG))).astype(BF16))
            xp = xdt[:, (g * 2 + hp) * LANES:(g * 2 + hp + 1) * LANES]
            rhs = jnp.concatenate([jnp.where(head0, xp, 0.0), jnp.where(head0, 0.0, xp)], axis=0).astype(BF16)
            y_diag.append(_dot(jnp.concatenate(gmats, axis=1), rhs))
        y_parts.append(jnp.concatenate(y_diag, axis=1) + y_off)
        st = _dot(xw[:, xg].T.astype(BF16), b_g)
        for j in range(HEADS_PER_GROUP):
            h = g * HEADS_PER_GROUP + j
            hs = slice(h * HEAD_DIM, (h + 1) * HEAD_DIM)
            dec = jnp.exp(cs128[CHUNK - 1:CHUNK, h * LANES:(h + 1) * LANES])
            h_sc[hs, :] = h_sc[hs, :] * dec + st[j * HEAD_DIM:(j + 1) * HEAD_DIM, :]

    y = jnp.concatenate(y_parts, axis=1) + dskip_ref[...] * xs
    y = y * _silu(z_ref[...].astype(F32)) if l_in == CHUNK else y[:l_in] * _silu(z_ref[...].astype(F32))
    y_ref[...] = _rms_norm(y, gn_ref[...]).astype(BF16)

    @pl.when(c == n_chunks - 1)
    def _():
        hout_ref[...] = h_sc[...]


def _ssd(xbc, dt, z, h0, pre, conv_w, conv_b, dtb_pad, a_pad, dskip64, gnorm):
    b, t, _ = xbc.shape
    l_in = min(CHUNK, t)
    nc = t // l_in
    tri = jnp.asarray(np.tril(np.ones((CHUNK, CHUNK), np.float32)), BF16)
    e128 = jnp.asarray(_expand_matrix(LANES), BF16)
    e64 = jnp.asarray(_expand_matrix(HEAD_DIM), BF16)
    step = lambda w: pl.BlockSpec((None, l_in, w), lambda bi, c: (bi, c, 0))
    per_b = lambda r, w: pl.BlockSpec((None, r, w), lambda bi, c: (bi, 0, 0))
    consts = [conv_w, conv_b, dtb_pad, a_pad, dskip64, gnorm, tri, e128, e64]
    return pl.pallas_call(
        functools.partial(_ssd_kernel, l_in=l_in, n_chunks=nc),
        grid=(b, nc),
        in_specs=[step(D_CONV), step(DT_PAD), step(D_SSM), per_b(D_SSM, SSM_STATE), per_b(8, D_CONV)]
                 + [_const_spec(a.shape) for a in consts],
        out_specs=[step(D_SSM), per_b(D_SSM, SSM_STATE)],
        out_shape=[jax.ShapeDtypeStruct((b, t, D_SSM), BF16),
                   jax.ShapeDtypeStruct((b, D_SSM, SSM_STATE), F32)],
        scratch_shapes=[pltpu.VMEM((CHUNK + 8, D_CONV), F32), pltpu.VMEM((D_SSM, SSM_STATE), F32)],
        compiler_params=_params(("arbitrary", "arbitrary"), VMEM_LIMIT),
        name="ssd",
    )(xbc, dt, z, h0, pre, *consts)


def _stat_expand_matrix():
    e = np.zeros((LANES, D_ATT), np.float32)
    for p in range(N_PAIRS):
        e[p, (2 * p) * HEAD_DIM:(2 * p + 1) * HEAD_DIM] = 1.0
        e[HEAD_DIM + p, (2 * p + 1) * HEAD_DIM:(2 * p + 2) * HEAD_DIM] = 1.0
    return e


def _out_proj_kernel(*refs, n_br, alpha):
    if n_br > 1:
        o_refs = refs[:n_br]
        st_refs = refs[n_br:2 * n_br]
        est_ref = refs[2 * n_br]
        rest = refs[2 * n_br + 1:]
        st = [r[...] for r in st_refs]
        m = st[0]
        for s in st[1:]:
            m = jnp.maximum(m, s)
        e = [jnp.exp2(s - m) for s in st]
        tot = e[0]
        for x in e[1:]:
            tot = tot + x
        inv = 1.0 / tot
        attn = None
        for o_r, ei in zip(o_refs, e):
            term = _dot_split(ei * inv, est_ref[...]) * o_r[...].astype(F32)
            attn = term if attn is None else attn + term
    else:
        attn = refs[0][...]
        rest = refs[1:]
    ys_ref, x_ref, ga_ref, wo_ref, g_ref, b_ref, out_ref = rest
    a = _rms_norm(attn, ga_ref[...]).astype(BF16)
    mix = _dot(a, wo_ref[0:D_ATT, :]) + _dot(ys_ref[...], wo_ref[D_ATT:, :])
    out_ref[...] = _layer_norm(alpha * x_ref[...] + mix, g_ref[...], b_ref[...])


def _out_proj(attn_in, y_ssm, x, g_attn, w_out, ln_g, ln_b, tm, alpha):
    r = x.shape[0]
    row = lambda w: pl.BlockSpec((tm, w), lambda i: (i, 0))
    if isinstance(attn_in, tuple):
        outs, stats = attn_in
        n_br = len(outs)
        est = jnp.asarray(_stat_expand_matrix(), BF16)
        args = list(outs) + list(stats) + [est]
        specs = [row(D_ATT)] * n_br + [row(LANES)] * n_br + [_const_spec(est.shape)]
    else:
        n_br = 1
        args = [attn_in]
        specs = [row(D_ATT)]
    args += [y_ssm, x, g_attn, w_out, ln_g, ln_b]
    specs += [row(D_SSM), row(D_MODEL), _const_spec(g_attn.shape), _const_spec(w_out.shape),
              _const_spec(ln_g.shape), _const_spec(ln_b.shape)]
    return pl.pallas_call(
        functools.partial(_out_proj_kernel, n_br=n_br, alpha=alpha),
        grid=(r // tm,),
        in_specs=specs,
        out_specs=row(D_MODEL),
        out_shape=jax.ShapeDtypeStruct((r, D_MODEL), F32),
        compiler_params=_params(("arbitrary",), VMEM_LIMIT),
        name="out_proj_ln",
    )(*args)


def _ffn_kernel(h_ref, wg_ref, wu_ref, wd_ref, g_ref, b_ref, out_ref, *, alpha):
    h = h_ref[...]
    hb = h.astype(BF16)
    acc = None
    for c in range(D_FF // FF_CHUNK):
        cols = slice(c * FF_CHUNK, (c + 1) * FF_CHUNK)
        act = (_silu(_dot(hb, wg_ref[:, cols])) * _dot(hb, wu_ref[:, cols])).astype(BF16)
        part = _dot(act, wd_ref[cols, :])
        acc = part if acc is None else acc + part
    out_ref[...] = _layer_norm(alpha * h + acc, g_ref[...], b_ref[...])


def _ffn(h, w_gate, w_up, w_down, ln_g, ln_b, tm, alpha):
    r = h.shape[0]
    row = pl.BlockSpec((tm, D_MODEL), lambda i: (i, 0))
    single = lambda a: pl.BlockSpec(a.shape, lambda i: (0, 0), pipeline_mode=pl.Buffered(1))
    return pl.pallas_call(
        functools.partial(_ffn_kernel, alpha=alpha),
        grid=(r // tm,),
        in_specs=[row, single(w_gate), single(w_up), single(w_down), _const_spec(ln_g.shape), _const_spec(ln_b.shape)],
        out_specs=row,
        out_shape=jax.ShapeDtypeStruct((r, D_MODEL), F32),
        compiler_params=_params(("arbitrary",), VMEM_LIMIT),
        name="ffn_ln",
    )(h, w_gate, w_up, w_down, ln_g, ln_b)


def _pad_lanes(v, width=LANES):
    return jnp.pad(v.astype(F32), (0, width - v.shape[0])).reshape(1, width)


def _layer_weights(w_in, conv_w, conv_b, dt_bias, a_log, d_skip, attn_norm_g, ssm_norm_g, w_out, ln1_g, ln1_b,
                   w_gate, w_up, w_down, ln2_g, ln2_b):
    row = lambda v: v.astype(F32).reshape(1, -1)
    return dict(
        w_in=jnp.pad(w_in, ((0, 0), (0, D_PROJ_PAD - w_in.shape[1]))).astype(BF16),
        conv_w=conv_w.astype(F32), conv_b=row(conv_b),
        dtb=_pad_lanes(dt_bias), a=_pad_lanes(-jnp.exp(a_log.astype(F32))),
        dskip=row(jnp.repeat(d_skip.astype(F32), HEAD_DIM)),
        g_attn=row(attn_norm_g), g_ssm=row(ssm_norm_g), w_out=w_out.astype(BF16),
        ln1_g=row(ln1_g), ln1_b=row(ln1_b), w_gate=w_gate.astype(BF16), w_up=w_up.astype(BF16),
        w_down=w_down.astype(BF16), ln2_g=row(ln2_g), ln2_b=row(ln2_b))


def _mix_and_ffn(attn_in, y_ssm, x2d, w, tm, alpha):
    hdn = _out_proj(attn_in, y_ssm, x2d, w["g_attn"], w["w_out"], w["ln1_g"], w["ln1_b"], tm, alpha)
    return _ffn(hdn, w["w_gate"], w["w_up"], w["w_down"], w["ln2_g"], w["ln2_b"], tm, alpha)


def _prompt_layer(x, w, alpha):
    b, t, _ = x.shape
    keep = min(MAX_WINDOW, t)
    rows = b * t
    q, kb, vb, kf, vf, z, xbc, dt = _in_proj(x, w["w_in"], 512, keep)
    outs, stats = [], []
    for _, dil in DILATED_BRANCHES:
        o, s = _attn_branch(q, kb, vb, dil)
        outs.append(o.reshape(rows, D_ATT))
        stats.append(s.reshape(rows, LANES))
    h0 = jnp.zeros((b, D_SSM, SSM_STATE), F32)
    pre = jnp.zeros((b, 8, D_CONV), F32)
    y_ssm, h_fin = _ssd(xbc, dt, z, h0, pre, w["conv_w"], w["conv_b"], w["dtb"], w["a"], w["dskip"], w["g_ssm"])
    y = _mix_and_ffn((outs, stats), y_ssm.reshape(rows, D_SSM), x.reshape(rows, D_MODEL), w, 512, alpha)
    conv_new = xbc[:, t - (CONV_WIDTH - 1):, :]
    return (y.reshape(b, t, D_MODEL), kf.reshape(b, keep, N_HEADS, HEAD_DIM), vf.reshape(b, keep, N_HEADS, HEAD_DIM),
            h_fin.reshape(b, N_HEADS, HEAD_DIM, SSM_STATE), conv_new)


def _sample_layer(x, cache_k, cache_v, state_ssm, state_conv, w, alpha):
    b, t, _ = x.shape
    rows = b * t
    wb = cache_k.shape[1]
    q, _, _, kf, vf, z, xbc, dt = _in_proj(x.reshape(1, rows, D_MODEL), w["w_in"], rows, rows)
    per_b = lambda a: a.reshape(b, t, a.shape[-1])
    kf, vf = per_b(kf), per_b(vf)
    attn = _sample_attn(per_b(q), kf, vf, cache_k.reshape(b, wb, D_ATT), cache_v.reshape(b, wb, D_ATT))
    pre = jnp.pad(state_conv.astype(F32), ((0, 0), (8 - (CONV_WIDTH - 1), 0), (0, 0)))
    xbc = per_b(xbc)
    y_ssm, h_fin = _ssd(xbc, per_b(dt), per_b(z), state_ssm.reshape(b, D_SSM, SSM_STATE).astype(F32), pre,
                        w["conv_w"], w["conv_b"], w["dtb"], w["a"], w["dskip"], w["g_ssm"])
    y = _mix_and_ffn(attn.reshape(rows, D_ATT), y_ssm.reshape(rows, D_SSM), x.reshape(rows, D_MODEL), w, rows, alpha)
    conv_new = jnp.concatenate([state_conv.astype(F32), xbc], axis=1)[:, -(CONV_WIDTH - 1):, :]
    return (y.reshape(b, t, D_MODEL), kf.reshape(b, t, N_HEADS, HEAD_DIM), vf.reshape(b, t, N_HEADS, HEAD_DIM),
            h_fin.reshape(b, N_HEADS, HEAD_DIM, SSM_STATE), conv_new)


def kernel(x_prompt, x_sample, cache_k_win, cache_v_win, state_ssm, state_conv, w_in, conv_w, conv_b, dt_bias, a_log, d_skip, attn_norm_g, ssm_norm_g, w_out, ln1_g, ln1_b, w_gate, w_up, w_down, ln2_g, ln2_b):
    depth = w_in.shape[0]
    alpha = (2.0 * depth) ** 0.25
    hp, hs = x_prompt, x_sample
    cols = [[] for _ in range(8)]
    for l in range(depth):
        w = _layer_weights(w_in[l], conv_w[l], conv_b[l], dt_bias[l], a_log[l], d_skip[l], attn_norm_g[l],
                           ssm_norm_g[l], w_out[l], ln1_g[l], ln1_b[l], w_gate[l], w_up[l], w_down[l],
                           ln2_g[l], ln2_b[l])
        hp, *prompt_state = _prompt_layer(hp, w, alpha)
        hs, *sample_state = _sample_layer(hs, cache_k_win[l], cache_v_win[l], state_ssm[l], state_conv[l], w, alpha)
        for dst, val in zip(cols, prompt_state + sample_state):
            dst.append(val)
    return (hp, hs) + tuple(jnp.stack(c) for c in cols)
```

```python
import functools
import math

import numpy as np
import jax
import jax.numpy as jnp
from jax import lax
from jax.experimental import pallas as pl
from jax.experimental.pallas import tpu as pltpu

F32 = jnp.float32
BF16 = jnp.bfloat16

D_MODEL = 1024
HEAD_DIM = 64
N_HEADS = 16
N_PAIRS = N_HEADS // 2
D_ATT = N_HEADS * HEAD_DIM
D_SSM = N_HEADS * HEAD_DIM
SSM_STATE = 128
N_GROUPS = 4
HEADS_PER_GROUP = N_HEADS // N_GROUPS
GROUP_W = HEADS_PER_GROUP * HEAD_DIM
CONV_WIDTH = 4
D_CONV = D_SSM + 2 * N_GROUPS * SSM_STATE
D_FF = 2816
FF_CHUNK = D_FF // 2
CHUNK = 128
DILATED_BRANCHES = ((128, 1), (512, 4), (2048, 16))
WIN = 128
MAX_WINDOW = 2048
LN_EPS = 1e-5
RMS_EPS = 1e-5
LANES = 128
DT_PAD = LANES
D_PROJ_PAD = 3 * D_ATT + D_SSM + D_CONV + DT_PAD
LOG2E = math.log2(math.e)
Q_SCALE = HEAD_DIM ** -0.5 * LOG2E
NEG = -1e30
VMEM_LIMIT = 56 * 1024 * 1024
ROW_TILE = 512
BLOCK_UNROLL = 16


def _alibi_slopes():
    return 2.0 ** (-8.0 * np.arange(1, N_HEADS + 1) / N_HEADS)


def _params(sem, vmem=None):
    return pltpu.CompilerParams(dimension_semantics=sem, vmem_limit_bytes=vmem)


def _const_spec(shape):
    nd = len(shape)
    return pl.BlockSpec(shape, lambda *_: (0,) * nd)


def _split_bf16(x):
    hi = x.astype(BF16)
    lo = (x - hi.astype(F32)).astype(BF16)
    return hi, lo


def _dot(a, b):
    return jnp.dot(a, b, preferred_element_type=F32)


def _dot_nt(a, b):
    return lax.dot_general(a, b, (((1,), (1,)), ((), ())), preferred_element_type=F32)


def _dot_split(x, e):
    hi, lo = _split_bf16(x)
    return _dot(hi, e) + _dot(lo, e)


def _layer_norm(x, g, b):
    mu = jnp.mean(x, axis=-1, keepdims=True)
    xc = x - mu
    var = jnp.mean(xc * xc, axis=-1, keepdims=True)
    return xc * lax.rsqrt(var + LN_EPS) * g + b


def _rms_norm(x, g):
    return x * lax.rsqrt(jnp.mean(x * x, axis=-1, keepdims=True) + RMS_EPS) * g


def _silu(x):
    return x * (1.0 / (1.0 + jnp.exp(-x)))


def _in_proj_kernel(x_ref, w_ref, *out_refs, pair_major):
    xb = x_ref[...].astype(BF16)

    def seg(lo, hi):
        return _dot(xb, w_ref[:, lo:hi])

    def put_pairs(ref, val):
        for hp in range(N_PAIRS):
            ref[hp] = val[:, hp * LANES:(hp + 1) * LANES]

    q = seg(0, D_ATT) * Q_SCALE
    k = seg(D_ATT, 2 * D_ATT)
    v = seg(2 * D_ATT, 3 * D_ATT)
    if pair_major:
        q_ref, kp_ref, vp_ref, kf_ref, vf_ref, z_ref, xbc_ref, dt_ref = out_refs
        put_pairs(q_ref, q)
        put_pairs(kp_ref, k)
        put_pairs(vp_ref, v)
    else:
        q_ref, kf_ref, vf_ref, z_ref, xbc_ref, dt_ref = out_refs
        q_ref[...] = q.astype(BF16)
    kf_ref[...] = k
    vf_ref[...] = v
    o = 3 * D_ATT
    z_ref[...] = seg(o, o + D_SSM).astype(BF16)
    o += D_SSM
    xbc_ref[...] = seg(o, o + D_CONV)
    o += D_CONV
    dt_ref[...] = seg(o, o + DT_PAD)


def _in_proj(x, w_pad, tm, keep, pair_major):
    b, t, _ = x.shape
    nt = t // tm
    skip = (t - keep) // tm
    row = lambda bi, i: (bi, i, 0)
    kept = lambda bi, i: (bi, jnp.maximum(i - skip, 0), 0)

    def spec(w, imap=row):
        return pl.BlockSpec((None, tm, w), imap)

    sds = jax.ShapeDtypeStruct
    pair_spec = pl.BlockSpec((None, N_PAIRS, tm, LANES), lambda bi, i: (bi, 0, i, 0))
    pair_shape = sds((b, N_PAIRS, t, LANES), F32)
    tail_specs = [spec(D_ATT, kept), spec(D_ATT, kept), spec(D_SSM), spec(D_CONV), spec(DT_PAD)]
    tail_shapes = [sds((b, keep, D_ATT), F32), sds((b, keep, D_ATT), F32),
                   sds((b, t, D_SSM), BF16), sds((b, t, D_CONV), F32), sds((b, t, DT_PAD), F32)]
    if pair_major:
        out_specs = [pair_spec] * 3 + tail_specs
        out_shape = [pair_shape] * 3 + tail_shapes
    else:
        out_specs = [spec(D_ATT)] + tail_specs
        out_shape = [sds((b, t, D_ATT), BF16)] + tail_shapes
    return pl.pallas_call(
        functools.partial(_in_proj_kernel, pair_major=pair_major),
        grid=(b, nt),
        in_specs=[spec(D_MODEL),
                  pl.BlockSpec((D_MODEL, D_PROJ_PAD), lambda bi, i: (0, 0), pipeline_mode=pl.Buffered(1))],
        out_specs=out_specs,
        out_shape=out_shape,
        compiler_params=_params(("arbitrary", "arbitrary"), VMEM_LIMIT),
        name="in_proj",
    )(x, w_pad)


def _branch_bias(dil):
    r = np.arange(WIN)[:, None]
    c = np.arange(2 * WIN)[None, :]
    ju = WIN + r - c
    valid = (ju >= 0) & (ju <= WIN)
    slopes = _alibi_slopes()[:, None, None]
    base = -slopes * (ju * dil)[None] * LOG2E
    full = np.where(valid[None], base, NEG)
    first = np.where((valid & (c >= WIN))[None], base, NEG)
    return np.stack([full, first]).astype(np.float32)


def _attn_block(qf, kblk, vblk, bias_ref, var, head0):
    q2 = jnp.concatenate([jnp.where(head0, qf, 0.0), jnp.where(head0, 0.0, qf)], axis=0).astype(BF16)
    s = _dot_nt(q2, kblk) + jnp.concatenate([bias_ref[var, 0], bias_ref[var, 1]], axis=0)
    m = jnp.max(s, axis=1, keepdims=True)
    p = jnp.exp2(s - m).astype(BF16)
    r0 = _dot(p[:WIN], vblk)
    r1 = _dot(p[WIN:], vblk)
    den = jnp.where(head0, r0[:, LANES:], r1[:, LANES:])
    num = jnp.where(head0, r0[:, :LANES], r1[:, :LANES])
    lse = jnp.where(head0, m[:WIN], m[WIN:]) + jnp.log2(den)
    return num / den, lse


def _attn_kernel(q_ref, k_ref, v_ref, b1_ref, b4_ref, b16_ref, o_ref,
                 q4_sc, k4_sc, v4_sc, kb1_sc, va1_sc, kb4_sc, va4_sc,
                 o1_sc, l1_sc, o4_sc, l4_sc, o16_sc, l16_sc, *, t):
    t4 = t // 4
    t16 = t // 16
    n1 = t // WIN
    n4 = t4 // WIN
    lane = lax.broadcasted_iota(jnp.int32, (1, LANES), 1)
    head0 = lane < HEAD_DIM
    ones = jnp.ones((WIN, LANES), BF16)
    zeros_k = jnp.zeros((WIN, LANES), BF16)
    zeros_v = jnp.zeros((WIN, 2 * LANES), BF16)

    def rows(start, n=WIN):
        return pl.ds(pl.multiple_of(start, WIN), n)

    def with_ones(vf):
        return jnp.concatenate([vf.astype(BF16), jnp.ones(vf.shape, BF16)], axis=1)

    kb1_sc[0:WIN, :] = zeros_k
    va1_sc[0:WIN, :] = zeros_v
    for c in range(4):
        kb4_sc[c, 0:WIN, :] = zeros_k
        va4_sc[c, 0:WIN, :] = zeros_v

    def fill1(j, carry):
        src = rows(j * WIN)
        dst = rows(WIN + j * WIN)
        kb1_sc[dst, :] = k_ref[src, :].astype(BF16)
        va1_sc[dst, :] = with_ones(v_ref[src, :])
        return carry

    lax.fori_loop(0, n1, fill1, 0, unroll=4)

    def fill4(c, carry):
        def body(j, carry):
            src = pl.ds(c + 4 * WIN * j, WIN, stride=4)
            dst = rows(c * t4 + j * WIN)
            kc = k_ref[src, :]
            vc = v_ref[src, :]
            k4_sc[dst, :] = kc
            v4_sc[dst, :] = vc
            q4_sc[dst, :] = q_ref[src, :]
            kb4_sc[c, rows(WIN + j * WIN), :] = kc.astype(BF16)
            va4_sc[c, rows(WIN + j * WIN), :] = with_ones(vc)
            return carry
        return lax.fori_loop(0, n4, body, carry, unroll=2)

    lax.fori_loop(0, 4, fill4, 0)

    def branch1(j, carry):
        var = jnp.where(j == 0, 1, 0)
        out, lse = _attn_block(q_ref[rows(j * WIN), :], kb1_sc[rows(j * WIN, 2 * WIN), :],
                               va1_sc[rows(j * WIN, 2 * WIN), :], b1_ref, var, head0)
        o1_sc[rows(j * WIN), :] = out
        l1_sc[rows(j * WIN), :] = lse
        return carry

    lax.fori_loop(0, n1, branch1, 0, unroll=BLOCK_UNROLL)

    def branch4(it, carry):
        c = it // n4
        j = it % n4
        var = jnp.where(j == 0, 1, 0)
        dst = rows(it * WIN)
        out, lse = _attn_block(q4_sc[dst, :], kb4_sc[c, rows(j * WIN, 2 * WIN), :],
                               va4_sc[c, rows(j * WIN, 2 * WIN), :], b4_ref, var, head0)
        o4_sc[dst, :] = out
        l4_sc[dst, :] = lse
        return carry

    lax.fori_loop(0, 4 * n4, branch4, 0, unroll=BLOCK_UNROLL)

    def branch16(cls, carry):
        r4 = cls // 4
        c = cls % 4
        base = r4 * t4 + c
        src = pl.ds(base, t16, stride=4)
        kfull = jnp.concatenate([zeros_k, k4_sc[src, :].astype(BF16)], axis=0)
        vfull = jnp.concatenate([zeros_v, with_ones(v4_sc[src, :])], axis=0)
        for j in range(t16 // WIN):
            qf = q4_sc[pl.ds(base + 4 * WIN * j, WIN, stride=4), :]
            out, lse = _attn_block(qf, kfull[j * WIN:(j + 2) * WIN], vfull[j * WIN:(j + 2) * WIN],
                                   b16_ref, 1 if j == 0 else 0, head0)
            dst = pl.ds(base + 4 * WIN * j, WIN, stride=4)
            o16_sc[dst, :] = out
            l16_sc[dst, :] = lse
        return carry

    lax.fori_loop(0, 16, branch16, 0, unroll=BLOCK_UNROLL // (t16 // WIN))

    def combine(c, carry):
        def body(j, carry):
            cm = rows(c * t4 + j * WIN)
            nat = pl.ds(c + 4 * WIN * j, WIN, stride=4)
            l1 = l1_sc[nat, :]
            l4 = l4_sc[cm, :]
            l16 = l16_sc[cm, :]
            m = jnp.maximum(jnp.maximum(l1, l4), l16)
            e1 = jnp.exp2(l1 - m)
            e4 = jnp.exp2(l4 - m)
            e16 = jnp.exp2(l16 - m)
            acc = e1 * o1_sc[nat, :] + e4 * o4_sc[cm, :] + e16 * o16_sc[cm, :]
            o_ref[nat, :] = acc / (e1 + e4 + e16)
            return carry
        return lax.fori_loop(0, n4, body, carry, unroll=2)

    lax.fori_loop(0, 4, combine, 0)


def _attn(q, k, v):
    b, npairs, t, _ = q.shape
    assert t % (16 * WIN) == 0
    slab = pl.BlockSpec((None, None, t, LANES), lambda bi, hp: (bi, hp, 0, 0))
    biases = [jnp.asarray(_branch_bias(dil)) for _, dil in DILATED_BRANCHES]
    bias_spec = pl.BlockSpec((2, 2, WIN, 2 * WIN), lambda bi, hp: (0, hp, 0, 0))
    t4 = t // 4
    f32_slab = pltpu.VMEM((t, LANES), F32)
    scratch = [f32_slab, f32_slab, f32_slab,
               pltpu.VMEM((t + WIN, LANES), BF16), pltpu.VMEM((t + WIN, 2 * LANES), BF16),
               pltpu.VMEM((4, t4 + WIN, LANES), BF16), pltpu.VMEM((4, t4 + WIN, 2 * LANES), BF16)] + [f32_slab] * 6
    return pl.pallas_call(
        functools.partial(_attn_kernel, t=t),
        grid=(b, npairs),
        in_specs=[slab, slab, slab, bias_spec, bias_spec, bias_spec],
        out_specs=slab,
        out_shape=jax.ShapeDtypeStruct((b, npairs, t, LANES), F32),
        scratch_shapes=scratch,
        compiler_params=_params(("arbitrary", "arbitrary"), VMEM_LIMIT),
        name="dilated_attn",
    )(q, k, v, *biases)


def _sample_table(wb, tn, kpad):
    i = np.arange(tn)[:, None]
    c = np.arange(wb + kpad)[None, :]
    dist = wb + i - c
    mult = np.zeros(dist.shape)
    for window, dil in DILATED_BRANCHES:
        mult += (dist >= 0) & (dist <= window) & (dist % dil == 0) & (c < wb + tn)
    logm = np.where(mult > 0, np.log2(np.maximum(mult, 1)), NEG)
    slopes = _alibi_slopes()[:, None, None]
    tab = np.where(mult[None] > 0, -slopes * dist[None] * LOG2E + logm[None], NEG)
    return tab.reshape(N_HEADS * tn, wb + kpad).astype(np.float32)


def _sample_attn_kernel(q_ref, kn_ref, vn_ref, kt_ref, vt_ref, tab_ref, o_ref, *, tn, kpad):
    rows = N_HEADS * tn
    wb = kt_ref.shape[-1]
    row_head = lax.shift_right_logical(lax.broadcasted_iota(jnp.int32, (rows, D_ATT), 0), int(math.log2(tn)))
    lane_head = lax.shift_right_logical(lax.broadcasted_iota(jnp.int32, (rows, D_ATT), 1), int(math.log2(HEAD_DIM)))
    diag = row_head == lane_head
    qt = jnp.concatenate([q_ref[...].astype(F32)] * N_HEADS, axis=0)
    qbd = jnp.where(diag, qt, 0.0).astype(BF16)
    pad = jnp.zeros((kpad - tn, D_ATT), F32)
    k_new = jnp.concatenate([kn_ref[...], pad], axis=0).astype(BF16)
    v_new = jnp.concatenate([vn_ref[...], pad], axis=0).astype(BF16)
    kt = kt_ref[...].reshape(D_ATT, wb).astype(BF16)
    vt = vt_ref[...].reshape(D_ATT, wb).astype(BF16)
    s = jnp.concatenate([_dot(qbd, kt), _dot_nt(qbd, k_new)], axis=1) + tab_ref[...]
    m = jnp.max(s, axis=1, keepdims=True)
    p = jnp.exp2(s - m).astype(BF16)
    den = jnp.sum(p.astype(F32), axis=1, keepdims=True)
    full = (_dot_nt(p[:, :wb], vt) + _dot(p[:, wb:], v_new)) / den
    full = jnp.where(diag, full, 0.0)
    acc = full[0:tn]
    for h in range(1, N_HEADS):
        acc = acc + full[h * tn:(h + 1) * tn]
    o_ref[...] = acc


def _sample_attn(q, k_new, v_new, cache_kt, cache_vt):
    b, tn, _ = q.shape
    wb = cache_kt.shape[-1]
    assert tn & (tn - 1) == 0
    kpad = LANES
    tab = jnp.asarray(_sample_table(wb, tn, kpad))
    new = pl.BlockSpec((None, tn, D_ATT), lambda bi: (bi, 0, 0))
    cache = pl.BlockSpec((None, N_HEADS, HEAD_DIM, wb), lambda bi: (bi, 0, 0, 0))
    return pl.pallas_call(
        functools.partial(_sample_attn_kernel, tn=tn, kpad=kpad),
        grid=(b,),
        in_specs=[new, new, new, cache, cache, _const_spec(tab.shape)],
        out_specs=new,
        out_shape=jax.ShapeDtypeStruct((b, tn, D_ATT), F32),
        compiler_params=_params(("arbitrary",), VMEM_LIMIT),
        name="sample_attn",
    )(q, k_new, v_new, cache_kt, cache_vt, tab)


def _expand_matrix(width):
    e = np.zeros((LANES, N_HEADS * width), np.float32)
    for h in range(N_HEADS):
        e[h, h * width:(h + 1) * width] = 1.0
    return e


def _ssd_kernel(xbc_ref, dt_ref, z_ref, h0_ref, pre_ref, convw_ref, convb_ref, dtb_ref, a_ref, dskip_ref,
                gn_ref, tri_ref, e128_ref, e64_ref, y_ref, hout_ref, xpad_sc, h_sc, *, l_in, n_chunks):
    c = pl.program_id(1)
    pre_rows = 8

    @pl.when(c == 0)
    def _():
        h_sc[...] = h0_ref[...]
        xpad_sc[0:pre_rows, :] = pre_ref[...]
        if l_in < CHUNK:
            xpad_sc[pre_rows + l_in:, :] = jnp.zeros((CHUNK - l_in, D_CONV), F32)

    xpad_sc[pre_rows:pre_rows + l_in, :] = xbc_ref[...]
    acc = convb_ref[...]
    for w in range(CONV_WIDTH):
        o = pre_rows - (CONV_WIDTH - 1) + w
        acc = acc + convw_ref[w:w + 1, :] * xpad_sc[o:o + CHUNK, :]
    xbc = _silu(acc)
    if n_chunks > 1:
        xpad_sc[0:pre_rows, :] = xpad_sc[CHUNK:CHUNK + pre_rows, :]

    xs = xbc[:, :D_SSM]
    bmat = xbc[:, D_SSM:D_SSM + N_GROUPS * SSM_STATE].astype(BF16)
    cmat = xbc[:, D_SSM + N_GROUPS * SSM_STATE:].astype(BF16)

    draw = dt_ref[...] + dtb_ref[...]
    dt = jnp.maximum(draw, 0.0) + jnp.log1p(jnp.exp(-jnp.abs(draw)))
    if l_in < CHUNK:
        dt = jnp.concatenate([dt, jnp.zeros((CHUNK - l_in, DT_PAD), F32)], axis=0)
    la = dt * a_ref[...]
    tri = tri_ref[...]
    la_hi, la_lo = _split_bf16(la)
    cs = _dot(tri, la_hi) + _dot(tri, la_lo)
    cs_t = cs.T
    cs128 = _dot_split(cs, e128_ref[...])
    cs64 = _dot_split(cs, e64_ref[...])
    dt64 = _dot_split(dt, e64_ref[...])
    xdt = xs * dt64
    decay_in = jnp.exp(cs64)
    cs_end = cs64[CHUNK - 1:CHUNK, :]
    xw = xdt * jnp.exp(cs_end - cs64)

    lane = lax.broadcasted_iota(jnp.int32, (1, LANES), 1)
    head0 = lane < HEAD_DIM
    causal = (lax.broadcasted_iota(jnp.int32, (CHUNK, CHUNK), 0)
              >= lax.broadcasted_iota(jnp.int32, (CHUNK, CHUNK), 1))

    y_parts = []
    for g in range(N_GROUPS):
        gs = slice(g * SSM_STATE, (g + 1) * SSM_STATE)
        xg = slice(g * GROUP_W, (g + 1) * GROUP_W)
        b_g = bmat[:, gs]
        c_g = cmat[:, gs]
        cb = _dot_nt(c_g, b_g)
        h_prev = h_sc[xg, :]
        y_off = _dot_nt(c_g, h_prev.astype(BF16)) * decay_in[:, xg]
        y_diag = []
        for hp in range(HEADS_PER_GROUP // 2):
            gmats = []
            for j in range(2):
                h = g * HEADS_PER_GROUP + 2 * hp + j
                seg = cs128[:, h * LANES:(h + 1) * LANES] - cs_t[h:h + 1, :]
                gmats.append((cb * jnp.exp(jnp.where(causal, seg, NEG))).astype(BF16))
            xp = xdt[:, (g * 2 + hp) * LANES:(g * 2 + hp + 1) * LANES]
            rhs = jnp.concatenate([jnp.where(head0, xp, 0.0), jnp.where(head0, 0.0, xp)], axis=0).astype(BF16)
            y_diag.append(_dot(jnp.concatenate(gmats, axis=1), rhs))
        y_parts.append(jnp.concatenate(y_diag, axis=1) + y_off)
        st = _dot(xw[:, xg].T.astype(BF16), b_g)
        for j in range(HEADS_PER_GROUP):
            h = g * HEADS_PER_GROUP + j
            hs = slice(h * HEAD_DIM, (h + 1) * HEAD_DIM)
            dec = jnp.exp(cs128[CHUNK - 1:CHUNK, h * LANES:(h + 1) * LANES])
            h_sc[hs, :] = h_sc[hs, :] * dec + st[j * HEAD_DIM:(j + 1) * HEAD_DIM, :]

    y = jnp.concatenate(y_parts, axis=1) + dskip_ref[...] * xs
    y = y * _silu(z_ref[...].astype(F32)) if l_in == CHUNK else y[:l_in] * _silu(z_ref[...].astype(F32))
    y_ref[...] = _rms_norm(y, gn_ref[...]).astype(BF16)

    @pl.when(c == n_chunks - 1)
    def _():
        hout_ref[...] = h_sc[...]


def _ssd(xbc, dt, z, h0, pre, conv_w, conv_b, dtb_pad, a_pad, dskip64, gnorm):
    b, t, _ = xbc.shape
    l_in = min(CHUNK, t)
    nc = t // l_in
    tri = jnp.asarray(np.tril(np.ones((CHUNK, CHUNK), np.float32)), BF16)
    e128 = jnp.asarray(_expand_matrix(LANES), BF16)
    e64 = jnp.asarray(_expand_matrix(HEAD_DIM), BF16)
    step = lambda w: pl.BlockSpec((None, l_in, w), lambda bi, c: (bi, c, 0))
    per_b = lambda r, w: pl.BlockSpec((None, r, w), lambda bi, c: (bi, 0, 0))
    consts = [conv_w, conv_b, dtb_pad, a_pad, dskip64, gnorm, tri, e128, e64]
    return pl.pallas_call(
        functools.partial(_ssd_kernel, l_in=l_in, n_chunks=nc),
        grid=(b, nc),
        in_specs=[step(D_CONV), step(DT_PAD), step(D_SSM), per_b(D_SSM, SSM_STATE), per_b(8, D_CONV)]
                 + [_const_spec(a.shape) for a in consts],
        out_specs=[step(D_SSM), per_b(D_SSM, SSM_STATE)],
        out_shape=[jax.ShapeDtypeStruct((b, t, D_SSM), BF16),
                   jax.ShapeDtypeStruct((b, D_SSM, SSM_STATE), F32)],
        scratch_shapes=[pltpu.VMEM((CHUNK + 8, D_CONV), F32), pltpu.VMEM((D_SSM, SSM_STATE), F32)],
        compiler_params=_params(("arbitrary", "arbitrary"), VMEM_LIMIT),
        name="ssd",
    )(xbc, dt, z, h0, pre, *consts)


def _out_proj_kernel(a_ref, ys_ref, x_ref, ga_ref, wo_ref, g_ref, b_ref, out_ref, *, pair_major, alpha):
    if pair_major:
        attn = jnp.concatenate([a_ref[hp] for hp in range(N_PAIRS)], axis=1)
    else:
        attn = a_ref[...]
    a = _rms_norm(attn, ga_ref[...]).astype(BF16)
    mix = _dot(a, wo_ref[0:D_ATT, :]) + _dot(ys_ref[...], wo_ref[D_ATT:, :])
    out_ref[...] = _layer_norm(alpha * x_ref[...] + mix, g_ref[...], b_ref[...])


def _out_proj(attn, y_ssm, x, g_attn, w_out, ln_g, ln_b, tm, alpha, pair_major):
    b, t, _ = x.shape
    row = lambda w: pl.BlockSpec((None, tm, w), lambda bi, i: (bi, i, 0))
    a_spec = pl.BlockSpec((None, N_PAIRS, tm, LANES), lambda bi, i: (bi, 0, i, 0)) if pair_major else row(D_ATT)
    consts = [g_attn, w_out, ln_g, ln_b]
    return pl.pallas_call(
        functools.partial(_out_proj_kernel, pair_major=pair_major, alpha=alpha),
        grid=(b, t // tm),
        in_specs=[a_spec, row(D_SSM), row(D_MODEL)] + [_const_spec(a.shape) for a in consts],
        out_specs=row(D_MODEL),
        out_shape=jax.ShapeDtypeStruct((b, t, D_MODEL), F32),
        compiler_params=_params(("arbitrary", "arbitrary"), VMEM_LIMIT),
        name="out_proj_ln",
    )(attn, y_ssm, x, *consts)


def _ffn_kernel(h_ref, wg_ref, wu_ref, wd_ref, g_ref, b_ref, out_ref, *, alpha):
    h = h_ref[...]
    hb = h.astype(BF16)
    acc = None
    for c in range(D_FF // FF_CHUNK):
        cols = slice(c * FF_CHUNK, (c + 1) * FF_CHUNK)
        act = (_silu(_dot(hb, wg_ref[:, cols])) * _dot(hb, wu_ref[:, cols])).astype(BF16)
        part = _dot(act, wd_ref[cols, :])
        acc = part if acc is None else acc + part
    out_ref[...] = _layer_norm(alpha * h + acc, g_ref[...], b_ref[...])


def _ffn(h, w_gate, w_up, w_down, ln_g, ln_b, tm, alpha):
    r = h.shape[0]
    row = pl.BlockSpec((tm, D_MODEL), lambda i: (i, 0))
    single = lambda a: pl.BlockSpec(a.shape, lambda i: (0, 0), pipeline_mode=pl.Buffered(1))
    return pl.pallas_call(
        functools.partial(_ffn_kernel, alpha=alpha),
        grid=(r // tm,),
        in_specs=[row, single(w_gate), single(w_up), single(w_down), _const_spec(ln_g.shape), _const_spec(ln_b.shape)],
        out_specs=row,
        out_shape=jax.ShapeDtypeStruct((r, D_MODEL), F32),
        compiler_params=_params(("arbitrary",), VMEM_LIMIT),
        name="ffn_ln",
    )(h, w_gate, w_up, w_down, ln_g, ln_b)


def _pad_lanes(v, width=LANES):
    return jnp.pad(v.astype(F32), (0, width - v.shape[0])).reshape(1, width)


def _layer_weights(w_in, conv_w, conv_b, dt_bias, a_log, d_skip, attn_norm_g, ssm_norm_g, w_out, ln1_g, ln1_b,
                   w_gate, w_up, w_down, ln2_g, ln2_b):
    row = lambda v: v.astype(F32).reshape(1, -1)
    return dict(
        w_in=jnp.pad(w_in, ((0, 0), (0, D_PROJ_PAD - w_in.shape[1]))).astype(BF16),
        conv_w=conv_w.astype(F32), conv_b=row(conv_b),
        dtb=_pad_lanes(dt_bias), a=_pad_lanes(-jnp.exp(a_log.astype(F32))),
        dskip=row(jnp.repeat(d_skip.astype(F32), HEAD_DIM)),
        g_attn=row(attn_norm_g), g_ssm=row(ssm_norm_g), w_out=w_out.astype(BF16),
        ln1_g=row(ln1_g), ln1_b=row(ln1_b), w_gate=w_gate.astype(BF16), w_up=w_up.astype(BF16),
        w_down=w_down.astype(BF16), ln2_g=row(ln2_g), ln2_b=row(ln2_b))


def _mix_and_ffn(attn, y_ssm, x, w, tm, alpha, pair_major):
    b, t, _ = x.shape
    hdn = _out_proj(attn, y_ssm, x, w["g_attn"], w["w_out"], w["ln1_g"], w["ln1_b"], tm, alpha, pair_major)
    y = _ffn(hdn.reshape(b * t, D_MODEL), w["w_gate"], w["w_up"], w["w_down"], w["ln2_g"], w["ln2_b"], tm, alpha)
    return y.reshape(b, t, D_MODEL)


def _ssd_weights(w):
    return w["conv_w"], w["conv_b"], w["dtb"], w["a"], w["dskip"], w["g_ssm"]


def _prompt_layer(x, w, alpha):
    b, t, _ = x.shape
    keep = min(MAX_WINDOW, t)
    q, kp, vp, kf, vf, z, xbc, dt = _in_proj(x, w["w_in"], ROW_TILE, keep, True)
    attn = _attn(q, kp, vp)
    h0 = jnp.zeros((b, D_SSM, SSM_STATE), F32)
    pre = jnp.zeros((b, 8, D_CONV), F32)
    y_ssm, h_fin = _ssd(xbc, dt, z, h0, pre, *_ssd_weights(w))
    y = _mix_and_ffn(attn, y_ssm, x, w, ROW_TILE, alpha, True)
    conv_new = xbc[:, t - (CONV_WIDTH - 1):, :]
    return (y, kf.reshape(b, keep, N_HEADS, HEAD_DIM), vf.reshape(b, keep, N_HEADS, HEAD_DIM),
            h_fin.reshape(b, N_HEADS, HEAD_DIM, SSM_STATE), conv_new)


def _sample_layer(x, cache_k, cache_v, state_ssm, state_conv, w, alpha):
    b, t, _ = x.shape
    rows = b * t
    flat = lambda a: a.reshape(1, rows, a.shape[-1])
    per_b = lambda a: a.reshape(b, t, a.shape[-1])
    q, kf, vf, z, xbc, dt = _in_proj(flat(x), w["w_in"], rows, rows, False)
    kf, vf, xbc = per_b(kf), per_b(vf), per_b(xbc)
    cache_t = lambda c: jnp.transpose(c, (0, 2, 3, 1))
    attn = _sample_attn(per_b(q), kf, vf, cache_t(cache_k), cache_t(cache_v))
    pre = jnp.pad(state_conv.astype(F32), ((0, 0), (8 - (CONV_WIDTH - 1), 0), (0, 0)))
    y_ssm, h_fin = _ssd(xbc, per_b(dt), per_b(z), state_ssm.reshape(b, D_SSM, SSM_STATE).astype(F32), pre,
                        *_ssd_weights(w))
    y = _mix_and_ffn(flat(attn), flat(y_ssm), flat(x), w, rows, alpha, False)
    conv_new = jnp.concatenate([state_conv.astype(F32), xbc], axis=1)[:, -(CONV_WIDTH - 1):, :]
    return (per_b(y[0]), kf.reshape(b, t, N_HEADS, HEAD_DIM), vf.reshape(b, t, N_HEADS, HEAD_DIM),
            h_fin.reshape(b, N_HEADS, HEAD_DIM, SSM_STATE), conv_new)


def kernel(x_prompt, x_sample, cache_k_win, cache_v_win, state_ssm, state_conv, w_in, conv_w, conv_b, dt_bias, a_log, d_skip, attn_norm_g, ssm_norm_g, w_out, ln1_g, ln1_b, w_gate, w_up, w_down, ln2_g, ln2_b):
    depth = w_in.shape[0]
    alpha = (2.0 * depth) ** 0.25
    hp, hs = x_prompt, x_sample
    cols = [[] for _ in range(8)]
    for l in range(depth):
        w = _layer_weights(w_in[l], conv_w[l], conv_b[l], dt_bias[l], a_log[l], d_skip[l], attn_norm_g[l],
                           ssm_norm_g[l], w_out[l], ln1_g[l], ln1_b[l], w_gate[l], w_up[l], w_down[l],
                           ln2_g[l], ln2_b[l])
        hp, *prompt_state = _prompt_layer(hp, w, alpha)
        hs, *sample_state = _sample_layer(hs, cache_k_win[l], cache_v_win[l], state_ssm[l], state_conv[l], w, alpha)
        for dst, val in zip(cols, prompt_state + sample_state):
            dst.append(val)
    return (hp, hs) + tuple(jnp.stack(c) for c in cols)
```

```python
import functools
import math

import numpy as np
import jax
import jax.numpy as jnp
from jax import lax
from jax.experimental import pallas as pl
from jax.experimental.pallas import tpu as pltpu

F32 = jnp.float32
BF16 = jnp.bfloat16

D_MODEL = 1024
HEAD_DIM = 64
N_HEADS = 16
N_PAIRS = N_HEADS // 2
D_ATT = N_HEADS * HEAD_DIM
D_SSM = N_HEADS * HEAD_DIM
SSM_STATE = 128
N_GROUPS = 4
HEADS_PER_GROUP = N_HEADS // N_GROUPS
GROUP_W = HEADS_PER_GROUP * HEAD_DIM
CONV_WIDTH = 4
D_CONV = D_SSM + 2 * N_GROUPS * SSM_STATE
D_FF = 2816
FF_CHUNK = D_FF // 2
CHUNK = 128
CONV_PRE = 8
CONV_PITCH = 2
SSD_GROUP = 2
DILATED_BRANCHES = ((128, 1), (512, 4), (2048, 16))
WIN = 128
MAX_WINDOW = 2048
LN_EPS = 1e-5
RMS_EPS = 1e-5
LANES = 128
DT_PAD = LANES
D_PROJ_PAD = 3 * D_ATT + D_SSM + D_CONV + DT_PAD
LOG2E = math.log2(math.e)
Q_SCALE = HEAD_DIM ** -0.5 * LOG2E
NEG = -1e30
VMEM_LIMIT = 56 * 1024 * 1024
ROW_TILE = 512
BLOCK_UNROLL = 16


def _alibi_slopes():
    return 2.0 ** (-8.0 * np.arange(1, N_HEADS + 1) / N_HEADS)


def _params(sem, vmem=None):
    return pltpu.CompilerParams(dimension_semantics=sem, vmem_limit_bytes=vmem)


def _const_spec(shape):
    nd = len(shape)
    return pl.BlockSpec(shape, lambda *_: (0,) * nd)


def _split_bf16(x):
    hi = x.astype(BF16)
    lo = (x - hi.astype(F32)).astype(BF16)
    return hi, lo


def _dot(a, b):
    return jnp.dot(a, b, preferred_element_type=F32)


def _dot_nt(a, b):
    return lax.dot_general(a, b, (((1,), (1,)), ((), ())), preferred_element_type=F32)


def _dot_split(x, e):
    hi, lo = _split_bf16(x)
    return _dot(hi, e) + _dot(lo, e)


def _layer_norm(x, g, b):
    mu = jnp.mean(x, axis=-1, keepdims=True)
    xc = x - mu
    var = jnp.mean(xc * xc, axis=-1, keepdims=True)
    return xc * lax.rsqrt(var + LN_EPS) * g + b


def _rms_norm(x, g):
    return x * lax.rsqrt(jnp.mean(x * x, axis=-1, keepdims=True) + RMS_EPS) * g


def _silu(x):
    return x * (1.0 / (1.0 + jnp.exp(-x)))


def _in_proj_kernel(x_ref, w_ref, *out_refs, pair_major):
    xb = x_ref[...].astype(BF16)

    def seg(lo, hi):
        return _dot(xb, w_ref[:, lo:hi])

    def put_pairs(ref, val):
        for hp in range(N_PAIRS):
            ref[hp] = val[:, hp * LANES:(hp + 1) * LANES]

    q = seg(0, D_ATT) * Q_SCALE
    k = seg(D_ATT, 2 * D_ATT)
    v = seg(2 * D_ATT, 3 * D_ATT)
    if pair_major:
        q_ref, kp_ref, vp_ref, kf_ref, vf_ref, z_ref, xbc_ref, dt_ref = out_refs
        put_pairs(q_ref, q)
        put_pairs(kp_ref, k)
        put_pairs(vp_ref, v)
    else:
        q_ref, kf_ref, vf_ref, z_ref, xbc_ref, dt_ref = out_refs
        q_ref[...] = q.astype(BF16)
    kf_ref[...] = k
    vf_ref[...] = v
    o = 3 * D_ATT
    z_ref[...] = seg(o, o + D_SSM).astype(BF16)
    o += D_SSM
    xbc_ref[...] = seg(o, o + D_CONV)
    o += D_CONV
    dt_ref[...] = seg(o, o + DT_PAD)


def _in_proj(x, w_pad, tm, keep, pair_major):
    b, t, _ = x.shape
    nt = t // tm
    skip = (t - keep) // tm
    row = lambda bi, i: (bi, i, 0)
    kept = lambda bi, i: (bi, jnp.maximum(i - skip, 0), 0)

    def spec(w, imap=row):
        return pl.BlockSpec((None, tm, w), imap)

    sds = jax.ShapeDtypeStruct
    pair_spec = pl.BlockSpec((None, N_PAIRS, tm, LANES), lambda bi, i: (bi, 0, i, 0))
    pair_shape = sds((b, N_PAIRS, t, LANES), F32)
    tail_specs = [spec(D_ATT, kept), spec(D_ATT, kept), spec(D_SSM), spec(D_CONV), spec(DT_PAD)]
    tail_shapes = [sds((b, keep, D_ATT), F32), sds((b, keep, D_ATT), F32),
                   sds((b, t, D_SSM), BF16), sds((b, t, D_CONV), F32), sds((b, t, DT_PAD), F32)]
    if pair_major:
        out_specs = [pair_spec] * 3 + tail_specs
        out_shape = [pair_shape] * 3 + tail_shapes
    else:
        out_specs = [spec(D_ATT)] + tail_specs
        out_shape = [sds((b, t, D_ATT), BF16)] + tail_shapes
    return pl.pallas_call(
        functools.partial(_in_proj_kernel, pair_major=pair_major),
        grid=(b, nt),
        in_specs=[spec(D_MODEL),
                  pl.BlockSpec((D_MODEL, D_PROJ_PAD), lambda bi, i: (0, 0), pipeline_mode=pl.Buffered(1))],
        out_specs=out_specs,
        out_shape=out_shape,
        compiler_params=_params(("arbitrary", "arbitrary"), VMEM_LIMIT),
        name="in_proj",
    )(x, w_pad)


def _branch_bias(dil):
    r = np.arange(WIN)[:, None]
    c = np.arange(2 * WIN)[None, :]
    ju = WIN + r - c
    valid = (ju >= 0) & (ju <= WIN)
    slopes = _alibi_slopes()[:, None, None]
    base = -slopes * (ju * dil)[None] * LOG2E
    full = np.where(valid[None], base, NEG)
    first = np.where((valid & (c >= WIN))[None], base, NEG)
    return np.stack([full, first]).astype(np.float32)


def _attn_block(qf, kblk, vblk, bias_ref, var, head0):
    q2 = jnp.concatenate([jnp.where(head0, qf, 0.0), jnp.where(head0, 0.0, qf)], axis=0).astype(BF16)
    s = _dot_nt(q2, kblk) + jnp.concatenate([bias_ref[var, 0], bias_ref[var, 1]], axis=0)
    m = jnp.max(s, axis=1, keepdims=True)
    p = jnp.exp2(s - m).astype(BF16)
    r0 = _dot(p[:WIN], vblk)
    r1 = _dot(p[WIN:], vblk)
    den = jnp.where(head0, r0[:, LANES:], r1[:, LANES:])
    num = jnp.where(head0, r0[:, :LANES], r1[:, :LANES])
    lse = jnp.where(head0, m[:WIN], m[WIN:]) + jnp.log2(den)
    return num / den, lse


def _attn_kernel(q_ref, k_ref, v_ref, b1_ref, b4_ref, b16_ref, o_ref,
                 q4_sc, k4_sc, v4_sc, kb1_sc, va1_sc, kb4_sc, va4_sc,
                 o1_sc, l1_sc, o4_sc, l4_sc, o16_sc, l16_sc, *, t):
    t4 = t // 4
    t16 = t // 16
    n1 = t // WIN
    n4 = t4 // WIN
    lane = lax.broadcasted_iota(jnp.int32, (1, LANES), 1)
    head0 = lane < HEAD_DIM
    ones = jnp.ones((WIN, LANES), BF16)
    zeros_k = jnp.zeros((WIN, LANES), BF16)
    zeros_v = jnp.zeros((WIN, 2 * LANES), BF16)

    def rows(start, n=WIN):
        return pl.ds(pl.multiple_of(start, WIN), n)

    def with_ones(vf):
        return jnp.concatenate([vf.astype(BF16), jnp.ones(vf.shape, BF16)], axis=1)

    kb1_sc[0:WIN, :] = zeros_k
    va1_sc[0:WIN, :] = zeros_v
    for c in range(4):
        kb4_sc[c, 0:WIN, :] = zeros_k
        va4_sc[c, 0:WIN, :] = zeros_v

    def fill1(j, carry):
        src = rows(j * WIN)
        dst = rows(WIN + j * WIN)
        kb1_sc[dst, :] = k_ref[src, :].astype(BF16)
        va1_sc[dst, :] = with_ones(v_ref[src, :])
        return carry

    lax.fori_loop(0, n1, fill1, 0, unroll=4)

    def fill4(c, carry):
        def body(j, carry):
            src = pl.ds(c + 4 * WIN * j, WIN, stride=4)
            dst = rows(c * t4 + j * WIN)
            kc = k_ref[src, :]
            vc = v_ref[src, :]
            k4_sc[dst, :] = kc
            v4_sc[dst, :] = vc
            q4_sc[dst, :] = q_ref[src, :]
            kb4_sc[c, rows(WIN + j * WIN), :] = kc.astype(BF16)
            va4_sc[c, rows(WIN + j * WIN), :] = with_ones(vc)
            return carry
        return lax.fori_loop(0, n4, body, carry, unroll=2)

    lax.fori_loop(0, 4, fill4, 0)

    def branch1(j, carry):
        var = jnp.where(j == 0, 1, 0)
        out, lse = _attn_block(q_ref[rows(j * WIN), :], kb1_sc[rows(j * WIN, 2 * WIN), :],
                               va1_sc[rows(j * WIN, 2 * WIN), :], b1_ref, var, head0)
        o1_sc[rows(j * WIN), :] = out
        l1_sc[rows(j * WIN), :] = lse
        return carry

    lax.fori_loop(0, n1, branch1, 0, unroll=BLOCK_UNROLL)

    def branch4(it, carry):
        c = it // n4
        j = it % n4
        var = jnp.where(j == 0, 1, 0)
        dst = rows(it * WIN)
        out, lse = _attn_block(q4_sc[dst, :], kb4_sc[c, rows(j * WIN, 2 * WIN), :],
                               va4_sc[c, rows(j * WIN, 2 * WIN), :], b4_ref, var, head0)
        o4_sc[dst, :] = out
        l4_sc[dst, :] = lse
        return carry

    lax.fori_loop(0, 4 * n4, branch4, 0, unroll=BLOCK_UNROLL)

    def branch16(cls, carry):
        r4 = cls // 4
        c = cls % 4
        base = r4 * t4 + c
        src = pl.ds(base, t16, stride=4)
        kfull = jnp.concatenate([zeros_k, k4_sc[src, :].astype(BF16)], axis=0)
        vfull = jnp.concatenate([zeros_v, with_ones(v4_sc[src, :])], axis=0)
        for j in range(t16 // WIN):
            qf = q4_sc[pl.ds(base + 4 * WIN * j, WIN, stride=4), :]
            out, lse = _attn_block(qf, kfull[j * WIN:(j + 2) * WIN], vfull[j * WIN:(j + 2) * WIN],
                                   b16_ref, 1 if j == 0 else 0, head0)
            dst = pl.ds(base + 4 * WIN * j, WIN, stride=4)
            o16_sc[dst, :] = out
            l16_sc[dst, :] = lse
        return carry

    lax.fori_loop(0, 16, branch16, 0, unroll=BLOCK_UNROLL // (t16 // WIN))

    def combine(c, carry):
        def body(j, carry):
            cm = rows(c * t4 + j * WIN)
            nat = pl.ds(c + 4 * WIN * j, WIN, stride=4)
            l1 = l1_sc[nat, :]
            l4 = l4_sc[cm, :]
            l16 = l16_sc[cm, :]
            m = jnp.maximum(jnp.maximum(l1, l4), l16)
            e1 = jnp.exp2(l1 - m)
            e4 = jnp.exp2(l4 - m)
            e16 = jnp.exp2(l16 - m)
            acc = e1 * o1_sc[nat, :] + e4 * o4_sc[cm, :] + e16 * o16_sc[cm, :]
            o_ref[nat, :] = acc / (e1 + e4 + e16)
            return carry
        return lax.fori_loop(0, n4, body, carry, unroll=2)

    lax.fori_loop(0, 4, combine, 0)


def _attn(q, k, v):
    b, npairs, t, _ = q.shape
    assert t % (16 * WIN) == 0
    slab = pl.BlockSpec((None, None, t, LANES), lambda bi, hp: (bi, hp, 0, 0))
    biases = [jnp.asarray(_branch_bias(dil)) for _, dil in DILATED_BRANCHES]
    bias_spec = pl.BlockSpec((2, 2, WIN, 2 * WIN), lambda bi, hp: (0, hp, 0, 0))
    t4 = t // 4
    f32_slab = pltpu.VMEM((t, LANES), F32)
    scratch = [f32_slab, f32_slab, f32_slab,
               pltpu.VMEM((t + WIN, LANES), BF16), pltpu.VMEM((t + WIN, 2 * LANES), BF16),
               pltpu.VMEM((4, t4 + WIN, LANES), BF16), pltpu.VMEM((4, t4 + WIN, 2 * LANES), BF16)] + [f32_slab] * 6
    return pl.pallas_call(
        functools.partial(_attn_kernel, t=t),
        grid=(b, npairs),
        in_specs=[slab, slab, slab, bias_spec, bias_spec, bias_spec],
        out_specs=slab,
        out_shape=jax.ShapeDtypeStruct((b, npairs, t, LANES), F32),
        scratch_shapes=scratch,
        compiler_params=_params(("arbitrary", "arbitrary"), VMEM_LIMIT),
        name="dilated_attn",
    )(q, k, v, *biases)


def _sample_table(wb, tn, kpad):
    i = np.arange(tn)[:, None]
    c = np.arange(wb + kpad)[None, :]
    dist = wb + i - c
    mult = np.zeros(dist.shape)
    for window, dil in DILATED_BRANCHES:
        mult += (dist >= 0) & (dist <= window) & (dist % dil == 0) & (c < wb + tn)
    logm = np.where(mult > 0, np.log2(np.maximum(mult, 1)), NEG)
    slopes = _alibi_slopes()[:, None, None]
    tab = np.where(mult[None] > 0, -slopes * dist[None] * LOG2E + logm[None], NEG)
    return tab.reshape(N_HEADS * tn, wb + kpad).astype(np.float32)


def _sample_attn_heads(q_ref, kn_ref, vn_ref, kt_ref, vt_ref, tab_ref, o_ref, *, tn):
    nh, _, wb = kt_ref.shape
    kpad = tab_ref.shape[1] - wb
    width = nh * HEAD_DIM
    rows = nh * tn
    row_head = lax.shift_right_logical(lax.broadcasted_iota(jnp.int32, (rows, width), 0), int(math.log2(tn)))
    lane_head = lax.shift_right_logical(lax.broadcasted_iota(jnp.int32, (rows, width), 1), int(math.log2(HEAD_DIM)))
    diag = row_head == lane_head
    qt = jnp.concatenate([q_ref[...].astype(F32)] * nh, axis=0)
    qbd = jnp.where(diag, qt, 0.0).astype(BF16)
    pad = jnp.zeros((kpad - tn, width), F32)
    k_new = jnp.concatenate([kn_ref[...], pad], axis=0).astype(BF16)
    v_new = jnp.concatenate([vn_ref[...], pad], axis=0).astype(BF16)
    kt = kt_ref[...].reshape(width, wb).astype(BF16)
    vt = vt_ref[...].reshape(width, wb).astype(BF16)
    s = jnp.concatenate([_dot(qbd, kt), _dot_nt(qbd, k_new)], axis=1) + tab_ref[...]
    m = jnp.max(s, axis=1, keepdims=True)
    p = jnp.exp2(s - m).astype(BF16)
    den = jnp.sum(p.astype(F32), axis=1, keepdims=True)
    full = (_dot_nt(p[:, :wb], vt) + _dot(p[:, wb:], v_new)) / den
    full = jnp.where(diag, full, 0.0)
    acc = full[0:tn]
    for h in range(1, nh):
        acc = acc + full[h * tn:(h + 1) * tn]
    o_ref[...] = acc


def _expand_matrix(width):
    e = np.zeros((LANES, N_HEADS * width), np.float32)
    for h in range(N_HEADS):
        e[h, h * width:(h + 1) * width] = 1.0
    return e


def _ssd_kernel(*refs, l_in, n_chunks, group, rider_tn):
    (xbc_ref, dt_ref, z_ref, h0_ref, pre_ref, convw_ref, convb_ref, dtb_ref, a_ref, dskip_ref,
     gn_ref, tri_ref, e128_ref, e64_ref) = refs[:14]
    if rider_tn:
        rider_in = refs[14:20]
        y_ref, hout_ref, rider_out, xpad_sc, tail_sc, h_sc = refs[20:]
        _sample_attn_heads(*rider_in, rider_out, tn=rider_tn)
    else:
        y_ref, hout_ref, xpad_sc, tail_sc, h_sc = refs[14:]
    c = pl.program_id(1)

    @pl.when(c == 0)
    def _():
        h_sc[...] = h0_ref[...]
        if l_in < CHUNK:
            xpad_sc[...] = jnp.zeros(xpad_sc.shape, F32)
        for g in range(group):
            for s in range(D_CONV // LANES):
                tail_sc[g, s] = pre_ref[g, :, s * LANES:(s + 1) * LANES]

    for g in range(group):
        _ssd_chunk(xbc_ref.at[g], dt_ref.at[g], z_ref.at[g], convw_ref, convb_ref, dtb_ref, a_ref, dskip_ref,
                   gn_ref, tri_ref, e128_ref, e64_ref, y_ref.at[g], xpad_sc.at[g], tail_sc.at[g], h_sc.at[g],
                   l_in=l_in, n_chunks=n_chunks)

    @pl.when(c == n_chunks - 1)
    def _():
        hout_ref[...] = h_sc[...]


def _ssd_chunk(xbc_ref, dt_ref, z_ref, convw_ref, convb_ref, dtb_ref, a_ref, dskip_ref, gn_ref, tri_ref, e128_ref,
               e64_ref, y_ref, xpad_sc, tail_sc, h_sc, *, l_in, n_chunks):
    n_slabs = D_CONV // LANES

    def time_rows(tau, n):
        return pl.ds(CONV_PITCH * (tau + CONV_PRE), n, stride=CONV_PITCH)

    slabs = []
    for s in range(n_slabs):
        lanes = slice(s * LANES, (s + 1) * LANES)
        xpad_sc[s, time_rows(-CONV_PRE, CONV_PRE), :] = tail_sc[s]
        xpad_sc[s, time_rows(0, l_in), :] = xbc_ref[:, lanes]
        acc = convb_ref[:, lanes]
        for w in range(CONV_WIDTH):
            acc = acc + convw_ref[w:w + 1, lanes] * xpad_sc[s, time_rows(w - (CONV_WIDTH - 1), CHUNK), :]
        slabs.append(_silu(acc))
        if n_chunks > 1:
            tail_sc[s] = xbc_ref[CHUNK - CONV_PRE:, lanes]
    n_x = D_SSM // LANES
    n_bc = N_GROUPS * SSM_STATE // LANES
    xs = jnp.concatenate(slabs[:n_x], axis=1)
    bmat = jnp.concatenate(slabs[n_x:n_x + n_bc], axis=1).astype(BF16)
    cmat = jnp.concatenate(slabs[n_x + n_bc:], axis=1).astype(BF16)

    draw = dt_ref[...] + dtb_ref[...]
    dt = jnp.maximum(draw, 0.0) + jnp.log1p(jnp.exp(-jnp.abs(draw)))
    if l_in < CHUNK:
        dt = jnp.concatenate([dt, jnp.zeros((CHUNK - l_in, DT_PAD), F32)], axis=0)
    la = dt * a_ref[...]
    tri = tri_ref[...]
    la_hi, la_lo = _split_bf16(la)
    cs = _dot(tri, la_hi) + _dot(tri, la_lo)
    cs_t = cs.T
    cs128 = _dot_split(cs, e128_ref[...])
    cs64 = _dot_split(cs, e64_ref[...])
    dt64 = _dot_split(dt, e64_ref[...])
    xdt = xs * dt64
    decay_in = jnp.exp(cs64)
    cs_end = cs64[CHUNK - 1:CHUNK, :]
    xw = xdt * jnp.exp(cs_end - cs64)

    lane = lax.broadcasted_iota(jnp.int32, (1, LANES), 1)
    head0 = lane < HEAD_DIM
    causal = (lax.broadcasted_iota(jnp.int32, (CHUNK, CHUNK), 0)
              >= lax.broadcasted_iota(jnp.int32, (CHUNK, CHUNK), 1))

    y_parts = []
    for g in range(N_GROUPS):
        gs = slice(g * SSM_STATE, (g + 1) * SSM_STATE)
        xg = slice(g * GROUP_W, (g + 1) * GROUP_W)
        b_g = bmat[:, gs]
        c_g = cmat[:, gs]
        cb = _dot_nt(c_g, b_g)
        h_prev = h_sc[xg, :]
        y_off = _dot_nt(c_g, h_prev.astype(BF16)) * decay_in[:, xg]
        y_diag = []
        for hp in range(HEADS_PER_GROUP // 2):
            gmats = []
            for j in range(2):
                h = g * HEADS_PER_GROUP + 2 * hp + j
                seg = cs128[:, h * LANES:(h + 1) * LANES] - cs_t[h:h + 1, :]
                gmats.append((cb * jnp.exp(jnp.where(causal, seg, NEG))).astype(BF16))
            xp = xdt[:, (g * 2 + hp) * LANES:(g * 2 + hp + 1) * LANES]
            rhs = jnp.concatenate([jnp.where(head0, xp, 0.0), jnp.where(head0, 0.0, xp)], axis=0).astype(BF16)
            y_diag.append(_dot(jnp.concatenate(gmats, axis=1), rhs))
        y_parts.append(jnp.concatenate(y_diag, axis=1) + y_off)
        st = _dot(xw[:, xg].T.astype(BF16), b_g)
        for j in range(HEADS_PER_GROUP):
            h = g * HEADS_PER_GROUP + j
            hs = slice(h * HEAD_DIM, (h + 1) * HEAD_DIM)
            dec = jnp.exp(cs128[CHUNK - 1:CHUNK, h * LANES:(h + 1) * LANES])
            h_sc[hs, :] = h_sc[hs, :] * dec + st[j * HEAD_DIM:(j + 1) * HEAD_DIM, :]

    y = jnp.concatenate(y_parts, axis=1) + dskip_ref[...] * xs
    y = y * _silu(z_ref[...].astype(F32)) if l_in == CHUNK else y[:l_in] * _silu(z_ref[...].astype(F32))
    y_ref[...] = _rms_norm(y, gn_ref[...]).astype(BF16)


def _ssd(xbc, dt, z, h0, pre, conv_w, conv_b, dtb_pad, a_pad, dskip64, gnorm, rider=None):
    b, t, _ = xbc.shape
    l_in = min(CHUNK, t)
    nc = t // l_in
    tri = jnp.asarray(np.tril(np.ones((CHUNK, CHUNK), np.float32)), BF16)
    e128 = jnp.asarray(_expand_matrix(LANES), BF16)
    e64 = jnp.asarray(_expand_matrix(HEAD_DIM), BF16)
    group = math.gcd(b, SSD_GROUP)
    step = lambda w: pl.BlockSpec((group, l_in, w), lambda bi, c: (bi, c, 0))
    per_b = lambda r, w: pl.BlockSpec((group, r, w), lambda bi, c: (bi, 0, 0))
    consts = [conv_w, conv_b, dtb_pad, a_pad, dskip64, gnorm, tri, e128, e64]
    args = [xbc, dt, z, h0, pre] + consts
    in_specs = ([step(D_CONV), step(DT_PAD), step(D_SSM), per_b(D_SSM, SSM_STATE), per_b(CONV_PRE, D_CONV)]
                + [_const_spec(a.shape) for a in consts])
    out_specs = [step(D_SSM), per_b(D_SSM, SSM_STATE)]
    out_shape = [jax.ShapeDtypeStruct((b, t, D_SSM), BF16), jax.ShapeDtypeStruct((b, D_SSM, SSM_STATE), F32)]
    rider_tn = 0
    if rider is not None:
        q, k_new, v_new, cache_kt, cache_vt = rider
        bs, rider_tn, _ = q.shape
        wb = cache_kt.shape[-1]
        steps = (b // group) * nc
        parts = steps // bs
        nh = N_HEADS // parts
        assert steps == parts * bs and nh * parts == N_HEADS and nh % 2 == 0 and rider_tn & (rider_tn - 1) == 0
        tab = jnp.asarray(_sample_table(wb, rider_tn, LANES))
        seq = lambda bi, c: (bi * nc + c) // parts
        part = lambda bi, c: (bi * nc + c) % parts
        new = pl.BlockSpec((None, rider_tn, nh * HEAD_DIM), lambda bi, c: (seq(bi, c), 0, part(bi, c)))
        cache = pl.BlockSpec((None, nh, HEAD_DIM, wb), lambda bi, c: (seq(bi, c), part(bi, c), 0, 0))
        args += [q, k_new, v_new, cache_kt, cache_vt, tab]
        in_specs += [new, new, new, cache, cache,
                     pl.BlockSpec((nh * rider_tn, tab.shape[1]), lambda bi, c: (part(bi, c), 0))]
        out_specs.append(new)
        out_shape.append(jax.ShapeDtypeStruct((bs, rider_tn, D_ATT), F32))
    return pl.pallas_call(
        functools.partial(_ssd_kernel, l_in=l_in, n_chunks=nc, group=group, rider_tn=rider_tn),
        grid=(b // group, nc),
        in_specs=in_specs,
        out_specs=out_specs,
        out_shape=out_shape,
        scratch_shapes=[pltpu.VMEM((group, D_CONV // LANES, CONV_PITCH * (CONV_PRE + CHUNK), LANES), F32),
                        pltpu.VMEM((group, D_CONV // LANES, CONV_PRE, LANES), F32),
                        pltpu.VMEM((group, D_SSM, SSM_STATE), F32)],
        compiler_params=_params(("arbitrary", "arbitrary"), VMEM_LIMIT),
        name="ssd",
    )(*args)


def _out_proj_kernel(a_ref, ys_ref, x_ref, ga_ref, wo_ref, g_ref, b_ref, out_ref, *, pair_major, alpha):
    if pair_major:
        attn = jnp.concatenate([a_ref[hp] for hp in range(N_PAIRS)], axis=1)
    else:
        attn = a_ref[...]
    a = _rms_norm(attn, ga_ref[...]).astype(BF16)
    mix = _dot(a, wo_ref[0:D_ATT, :]) + _dot(ys_ref[...], wo_ref[D_ATT:, :])
    out_ref[...] = _layer_norm(alpha * x_ref[...] + mix, g_ref[...], b_ref[...])


def _out_proj(attn, y_ssm, x, g_attn, w_out, ln_g, ln_b, tm, alpha, pair_major):
    b, t, _ = x.shape
    row = lambda w: pl.BlockSpec((None, tm, w), lambda bi, i: (bi, i, 0))
    a_spec = pl.BlockSpec((None, N_PAIRS, tm, LANES), lambda bi, i: (bi, 0, i, 0)) if pair_major else row(D_ATT)
    consts = [g_attn, w_out, ln_g, ln_b]
    return pl.pallas_call(
        functools.partial(_out_proj_kernel, pair_major=pair_major, alpha=alpha),
        grid=(b, t // tm),
        in_specs=[a_spec, row(D_SSM), row(D_MODEL)] + [_const_spec(a.shape) for a in consts],
        out_specs=row(D_MODEL),
        out_shape=jax.ShapeDtypeStruct((b, t, D_MODEL), F32),
        compiler_params=_params(("arbitrary", "arbitrary"), VMEM_LIMIT),
        name="out_proj_ln",
    )(attn, y_ssm, x, *consts)


def _ffn_kernel(h_ref, wg_ref, wu_ref, wd_ref, g_ref, b_ref, out_ref, *, alpha):
    h = h_ref[...]
    hb = h.astype(BF16)
    acc = None
    for c in range(D_FF // FF_CHUNK):
        cols = slice(c * FF_CHUNK, (c + 1) * FF_CHUNK)
        act = (_silu(_dot(hb, wg_ref[:, cols])) * _dot(hb, wu_ref[:, cols])).astype(BF16)
        part = _dot(act, wd_ref[cols, :])
        acc = part if acc is None else acc + part
    out_ref[...] = _layer_norm(alpha * h + acc, g_ref[...], b_ref[...])


def _ffn(h, w_gate, w_up, w_down, ln_g, ln_b, tm, alpha):
    r = h.shape[0]
    row = pl.BlockSpec((tm, D_MODEL), lambda i: (i, 0))
    single = lambda a: pl.BlockSpec(a.shape, lambda i: (0, 0), pipeline_mode=pl.Buffered(1))
    return pl.pallas_call(
        functools.partial(_ffn_kernel, alpha=alpha),
        grid=(r // tm,),
        in_specs=[row, single(w_gate), single(w_up), single(w_down), _const_spec(ln_g.shape), _const_spec(ln_b.shape)],
        out_specs=row,
        out_shape=jax.ShapeDtypeStruct((r, D_MODEL), F32),
        compiler_params=_params(("arbitrary",), VMEM_LIMIT),
        name="ffn_ln",
    )(h, w_gate, w_up, w_down, ln_g, ln_b)


def _pad_lanes(v, width=LANES):
    return jnp.pad(v.astype(F32), (0, width - v.shape[0])).reshape(1, width)


def _layer_weights(w_in, conv_w, conv_b, dt_bias, a_log, d_skip, attn_norm_g, ssm_norm_g, w_out, ln1_g, ln1_b,
                   w_gate, w_up, w_down, ln2_g, ln2_b):
    row = lambda v: v.astype(F32).reshape(1, -1)
    return dict(
        w_in=jnp.pad(w_in, ((0, 0), (0, D_PROJ_PAD - w_in.shape[1]))).astype(BF16),
        conv_w=conv_w.astype(F32), conv_b=row(conv_b),
        dtb=_pad_lanes(dt_bias), a=_pad_lanes(-jnp.exp(a_log.astype(F32))),
        dskip=row(jnp.repeat(d_skip.astype(F32), HEAD_DIM)),
        g_attn=row(attn_norm_g), g_ssm=row(ssm_norm_g), w_out=w_out.astype(BF16),
        ln1_g=row(ln1_g), ln1_b=row(ln1_b), w_gate=w_gate.astype(BF16), w_up=w_up.astype(BF16),
        w_down=w_down.astype(BF16), ln2_g=row(ln2_g), ln2_b=row(ln2_b))


def _mix_and_ffn(attn, y_ssm, x, w, tm, alpha, pair_major):
    b, t, _ = x.shape
    hdn = _out_proj(attn, y_ssm, x, w["g_attn"], w["w_out"], w["ln1_g"], w["ln1_b"], tm, alpha, pair_major)
    y = _ffn(hdn.reshape(b * t, D_MODEL), w["w_gate"], w["w_up"], w["w_down"], w["ln2_g"], w["ln2_b"], tm, alpha)
    return y.reshape(b, t, D_MODEL)


def _ssd_weights(w):
    return w["conv_w"], w["conv_b"], w["dtb"], w["a"], w["dskip"], w["g_ssm"]


def _layer(xp, xs, cache_k, cache_v, state_ssm, state_conv, w, alpha):
    bp, t, _ = xp.shape
    bs, tn, _ = xs.shape
    keep = min(MAX_WINDOW, t)
    rows = bs * tn
    flat = lambda a: a.reshape(1, rows, a.shape[-1])
    per_b = lambda a: a.reshape(bs, tn, a.shape[-1])
    heads = lambda a: a.reshape(a.shape[0], a.shape[1], N_HEADS, HEAD_DIM)
    state = lambda h: h.reshape(h.shape[0], N_HEADS, HEAD_DIM, SSM_STATE)

    q_s, kf_s, vf_s, z_s, xbc_s, dt_s = _in_proj(flat(xs), w["w_in"], rows, rows, False)
    kf_s, vf_s, xbc_s = per_b(kf_s), per_b(vf_s), per_b(xbc_s)
    q, kp, vp, kf, vf, z, xbc, dt = _in_proj(xp, w["w_in"], ROW_TILE, keep, True)
    attn = _attn(q, kp, vp)

    cache_t = lambda c: jnp.transpose(c, (0, 2, 3, 1))
    rider = (per_b(q_s), kf_s, vf_s, cache_t(cache_k), cache_t(cache_v))
    h0 = jnp.zeros((bp, D_SSM, SSM_STATE), F32)
    pre = jnp.zeros((bp, CONV_PRE, D_CONV), F32)
    y_ssm, h_fin, attn_s = _ssd(xbc, dt, z, h0, pre, *_ssd_weights(w), rider=rider)
    yp = _mix_and_ffn(attn, y_ssm, xp, w, ROW_TILE, alpha, True)
    conv_new = xbc[:, t - (CONV_WIDTH - 1):, :]

    pre_s = jnp.pad(state_conv.astype(F32), ((0, 0), (CONV_PRE - (CONV_WIDTH - 1), 0), (0, 0)))
    y_ssm_s, h_fin_s = _ssd(xbc_s, per_b(dt_s), per_b(z_s), state_ssm.reshape(bs, D_SSM, SSM_STATE).astype(F32),
                            pre_s, *_ssd_weights(w))
    ys = _mix_and_ffn(flat(attn_s), flat(y_ssm_s), flat(xs), w, rows, alpha, False)
    conv_new_s = jnp.concatenate([state_conv.astype(F32), xbc_s], axis=1)[:, -(CONV_WIDTH - 1):, :]
    return (yp, per_b(ys[0]), heads(kf), heads(vf), state(h_fin), conv_new,
            heads(kf_s), heads(vf_s), state(h_fin_s), conv_new_s)


def kernel(x_prompt, x_sample, cache_k_win, cache_v_win, state_ssm, state_conv, w_in, conv_w, conv_b, dt_bias, a_log, d_skip, attn_norm_g, ssm_norm_g, w_out, ln1_g, ln1_b, w_gate, w_up, w_down, ln2_g, ln2_b):
    depth = w_in.shape[0]
    alpha = (2.0 * depth) ** 0.25
    hp, hs = x_prompt, x_sample
    cols = [[] for _ in range(8)]
    for l in range(depth):
        w = _layer_weights(w_in[l], conv_w[l], conv_b[l], dt_bias[l], a_log[l], d_skip[l], attn_norm_g[l],
                           ssm_norm_g[l], w_out[l], ln1_g[l], ln1_b[l], w_gate[l], w_up[l], w_down[l],
                           ln2_g[l], ln2_b[l])
        hp, hs, *layer_state = _layer(hp, hs, cache_k_win[l], cache_v_win[l], state_ssm[l], state_conv[l], w, alpha)
        for dst, val in zip(cols, layer_state):
            dst.append(val)
    return (hp, hs) + tuple(jnp.stack(c) for c in cols)
```

```python
import functools
import math

import numpy as np
import jax
import jax.numpy as jnp
from jax import lax
from jax.experimental import pallas as pl
from jax.experimental.pallas import tpu as pltpu

F32 = jnp.float32
BF16 = jnp.bfloat16

D_MODEL = 1024
HEAD_DIM = 64
N_HEADS = 16
N_PAIRS = N_HEADS // 2
D_ATT = N_HEADS * HEAD_DIM
D_SSM = N_HEADS * HEAD_DIM
SSM_STATE = 128
N_GROUPS = 4
HEADS_PER_GROUP = N_HEADS // N_GROUPS
GROUP_W = HEADS_PER_GROUP * HEAD_DIM
CONV_WIDTH = 4
D_CONV = D_SSM + 2 * N_GROUPS * SSM_STATE
D_FF = 2816
FF_CHUNK = D_FF // 2
CHUNK = 128
CONV_PRE = 8
CONV_PITCH = 2
SSD_GROUP = 2
DILATED_BRANCHES = ((128, 1), (512, 4), (2048, 16))
WIN = 128
MAX_WINDOW = 2048
LN_EPS = 1e-5
RMS_EPS = 1e-5
LANES = 128
DT_PAD = LANES
D_PROJ_PAD = 3 * D_ATT + D_SSM + D_CONV + DT_PAD
LOG2E = math.log2(math.e)
Q_SCALE = HEAD_DIM ** -0.5 * LOG2E
NEG = -1e30
VMEM_LIMIT = 56 * 1024 * 1024
ROW_TILE = 512
BLOCK_UNROLL = 32


def _alibi_slopes():
    return 2.0 ** (-8.0 * np.arange(1, N_HEADS + 1) / N_HEADS)


def _params(sem, vmem=None):
    return pltpu.CompilerParams(dimension_semantics=sem, vmem_limit_bytes=vmem)


def _const_spec(shape):
    nd = len(shape)
    return pl.BlockSpec(shape, lambda *_: (0,) * nd)


def _split_bf16(x):
    hi = x.astype(BF16)
    lo = (x - hi.astype(F32)).astype(BF16)
    return hi, lo


def _dot(a, b):
    return jnp.dot(a, b, preferred_element_type=F32)


def _dot_nt(a, b):
    return lax.dot_general(a, b, (((1,), (1,)), ((), ())), preferred_element_type=F32)


def _dot_split(x, e):
    hi, lo = _split_bf16(x)
    return _dot(hi, e) + _dot(lo, e)


def _layer_norm(x, g, b):
    mu = jnp.mean(x, axis=-1, keepdims=True)
    xc = x - mu
    var = jnp.mean(xc * xc, axis=-1, keepdims=True)
    return xc * lax.rsqrt(var + LN_EPS) * g + b


def _rms_norm(x, g):
    return x * lax.rsqrt(jnp.mean(x * x, axis=-1, keepdims=True) + RMS_EPS) * g


def _silu(x):
    return x * (1.0 / (1.0 + jnp.exp(-x)))


def _in_proj_kernel(x_ref, w_ref, *out_refs, pair_major):
    xb = x_ref[...].astype(BF16)

    def seg(lo, hi):
        return _dot(xb, w_ref[:, lo:hi])

    def put_pairs(ref, val):
        for hp in range(N_PAIRS):
            ref[hp] = val[:, hp * LANES:(hp + 1) * LANES]

    q = seg(0, D_ATT) * Q_SCALE
    k = seg(D_ATT, 2 * D_ATT)
    v = seg(2 * D_ATT, 3 * D_ATT)
    if pair_major:
        q_ref, kp_ref, vp_ref, kf_ref, vf_ref, z_ref, xbc_ref, dt_ref = out_refs
        put_pairs(q_ref, q)
        put_pairs(kp_ref, k)
        put_pairs(vp_ref, v)
    else:
        q_ref, kf_ref, vf_ref, z_ref, xbc_ref, dt_ref = out_refs
        q_ref[...] = q.astype(BF16)
    kf_ref[...] = k
    vf_ref[...] = v
    o = 3 * D_ATT
    z_ref[...] = seg(o, o + D_SSM).astype(BF16)
    o += D_SSM
    xbc_ref[...] = seg(o, o + D_CONV)
    o += D_CONV
    dt_ref[...] = seg(o, o + DT_PAD)


def _in_proj(x, w_pad, tm, keep, pair_major):
    b, t, _ = x.shape
    nt = t // tm
    skip = (t - keep) // tm
    row = lambda bi, i: (bi, i, 0)
    kept = lambda bi, i: (bi, jnp.maximum(i - skip, 0), 0)

    def spec(w, imap=row):
        return pl.BlockSpec((None, tm, w), imap)

    sds = jax.ShapeDtypeStruct
    pair_spec = pl.BlockSpec((None, N_PAIRS, tm, LANES), lambda bi, i: (bi, 0, i, 0))
    pair_shape = sds((b, N_PAIRS, t, LANES), F32)
    tail_specs = [spec(D_ATT, kept), spec(D_ATT, kept), spec(D_SSM), spec(D_CONV), spec(DT_PAD)]
    tail_shapes = [sds((b, keep, D_ATT), F32), sds((b, keep, D_ATT), F32),
                   sds((b, t, D_SSM), BF16), sds((b, t, D_CONV), F32), sds((b, t, DT_PAD), F32)]
    if pair_major:
        out_specs = [pair_spec] * 3 + tail_specs
        out_shape = [pair_shape] * 3 + tail_shapes
    else:
        out_specs = [spec(D_ATT)] + tail_specs
        out_shape = [sds((b, t, D_ATT), BF16)] + tail_shapes
    return pl.pallas_call(
        functools.partial(_in_proj_kernel, pair_major=pair_major),
        grid=(b, nt),
        in_specs=[spec(D_MODEL),
                  pl.BlockSpec((D_MODEL, D_PROJ_PAD), lambda bi, i: (0, 0), pipeline_mode=pl.Buffered(1))],
        out_specs=out_specs,
        out_shape=out_shape,
        compiler_params=_params(("arbitrary", "arbitrary"), VMEM_LIMIT),
        name="in_proj",
    )(x, w_pad)


def _branch_bias(dil):
    r = np.arange(WIN)[:, None]
    c = np.arange(2 * WIN)[None, :]
    ju = WIN + r - c
    valid = (ju >= 0) & (ju <= WIN)
    slopes = _alibi_slopes()[:, None, None]
    base = -slopes * (ju * dil)[None] * LOG2E
    full = np.where(valid[None], base, NEG)
    first = np.where((valid & (c >= WIN))[None], base, NEG)
    return np.stack([full, first]).astype(np.float32)


def _attn_block(qf, kblk, vblk, bias_ref, var, head0):
    q2 = jnp.concatenate([jnp.where(head0, qf, 0.0), jnp.where(head0, 0.0, qf)], axis=0).astype(BF16)
    s = _dot_nt(q2, kblk) + jnp.concatenate([bias_ref[var, 0], bias_ref[var, 1]], axis=0)
    m = jnp.max(s, axis=1, keepdims=True)
    p = jnp.exp2(s - m).astype(BF16)
    r0 = _dot(p[:WIN], vblk)
    r1 = _dot(p[WIN:], vblk)
    den = jnp.where(head0, r0[:, LANES:], r1[:, LANES:])
    num = jnp.where(head0, r0[:, :LANES], r1[:, :LANES])
    lse = jnp.where(head0, m[:WIN], m[WIN:]) + jnp.log2(den)
    return num / den, lse


def _attn_kernel(q_ref, k_ref, v_ref, b1_ref, b4_ref, b16_ref, o_ref,
                 q4_sc, k4_sc, v4_sc, kb1_sc, va1_sc, kb4_sc, va4_sc,
                 o1_sc, l1_sc, o4_sc, l4_sc, o16_sc, l16_sc, *, t):
    t4 = t // 4
    t16 = t // 16
    n1 = t // WIN
    n4 = t4 // WIN
    lane = lax.broadcasted_iota(jnp.int32, (1, LANES), 1)
    head0 = lane < HEAD_DIM
    ones = jnp.ones((WIN, LANES), BF16)
    zeros_k = jnp.zeros((WIN, LANES), BF16)
    zeros_v = jnp.zeros((WIN, 2 * LANES), BF16)

    def rows(start, n=WIN):
        return pl.ds(pl.multiple_of(start, WIN), n)

    def with_ones(vf):
        return jnp.concatenate([vf.astype(BF16), jnp.ones(vf.shape, BF16)], axis=1)

    kb1_sc[0:WIN, :] = zeros_k
    va1_sc[0:WIN, :] = zeros_v
    for c in range(4):
        kb4_sc[c, 0:WIN, :] = zeros_k
        va4_sc[c, 0:WIN, :] = zeros_v

    def fill1(j, carry):
        src = rows(j * WIN)
        dst = rows(WIN + j * WIN)
        kb1_sc[dst, :] = k_ref[src, :].astype(BF16)
        va1_sc[dst, :] = with_ones(v_ref[src, :])
        return carry

    lax.fori_loop(0, n1, fill1, 0, unroll=4)

    def fill4(c, carry):
        def body(j, carry):
            src = pl.ds(c + 4 * WIN * j, WIN, stride=4)
            dst = rows(c * t4 + j * WIN)
            kc = k_ref[src, :]
            vc = v_ref[src, :]
            k4_sc[dst, :] = kc
            v4_sc[dst, :] = vc
            q4_sc[dst, :] = q_ref[src, :]
            kb4_sc[c, rows(WIN + j * WIN), :] = kc.astype(BF16)
            va4_sc[c, rows(WIN + j * WIN), :] = with_ones(vc)
            return carry
        return lax.fori_loop(0, n4, body, carry, unroll=2)

    lax.fori_loop(0, 4, fill4, 0)

    def branch1(j, carry):
        var = jnp.where(j == 0, 1, 0)
        out, lse = _attn_block(q_ref[rows(j * WIN), :], kb1_sc[rows(j * WIN, 2 * WIN), :],
                               va1_sc[rows(j * WIN, 2 * WIN), :], b1_ref, var, head0)
        o1_sc[rows(j * WIN), :] = out
        l1_sc[rows(j * WIN), :] = lse
        return carry

    lax.fori_loop(0, n1, branch1, 0, unroll=min(n1, BLOCK_UNROLL))

    def branch4(it, carry):
        c = it // n4
        j = it % n4
        var = jnp.where(j == 0, 1, 0)
        dst = rows(it * WIN)
        out, lse = _attn_block(q4_sc[dst, :], kb4_sc[c, rows(j * WIN, 2 * WIN), :],
                               va4_sc[c, rows(j * WIN, 2 * WIN), :], b4_ref, var, head0)
        o4_sc[dst, :] = out
        l4_sc[dst, :] = lse
        return carry

    lax.fori_loop(0, 4 * n4, branch4, 0, unroll=min(4 * n4, BLOCK_UNROLL))

    def branch16(cls, carry):
        r4 = cls // 4
        c = cls % 4
        base = r4 * t4 + c
        src = pl.ds(base, t16, stride=4)
        kfull = jnp.concatenate([zeros_k, k4_sc[src, :].astype(BF16)], axis=0)
        vfull = jnp.concatenate([zeros_v, with_ones(v4_sc[src, :])], axis=0)
        for j in range(t16 // WIN):
            qf = q4_sc[pl.ds(base + 4 * WIN * j, WIN, stride=4), :]
            out, lse = _attn_block(qf, kfull[j * WIN:(j + 2) * WIN], vfull[j * WIN:(j + 2) * WIN],
                                   b16_ref, 1 if j == 0 else 0, head0)
            dst = pl.ds(base + 4 * WIN * j, WIN, stride=4)
            o16_sc[dst, :] = out
            l16_sc[dst, :] = lse
        return carry

    lax.fori_loop(0, 16, branch16, 0, unroll=min(16, BLOCK_UNROLL // (t16 // WIN)))

    def combine(c, carry):
        def body(j, carry):
            cm = rows(c * t4 + j * WIN)
            nat = pl.ds(c + 4 * WIN * j, WIN, stride=4)
            l1 = l1_sc[nat, :]
            l4 = l4_sc[cm, :]
            l16 = l16_sc[cm, :]
            m = jnp.maximum(jnp.maximum(l1, l4), l16)
            e1 = jnp.exp2(l1 - m)
            e4 = jnp.exp2(l4 - m)
            e16 = jnp.exp2(l16 - m)
            acc = e1 * o1_sc[nat, :] + e4 * o4_sc[cm, :] + e16 * o16_sc[cm, :]
            o_ref[nat, :] = acc / (e1 + e4 + e16)
            return carry
        return lax.fori_loop(0, n4, body, carry, unroll=2)

    lax.fori_loop(0, 4, combine, 0)


def _attn(q, k, v):
    b, npairs, t, _ = q.shape
    assert t % (16 * WIN) == 0
    slab = pl.BlockSpec((None, None, t, LANES), lambda bi, hp: (bi, hp, 0, 0))
    biases = [jnp.asarray(_branch_bias(dil)) for _, dil in DILATED_BRANCHES]
    bias_spec = pl.BlockSpec((2, 2, WIN, 2 * WIN), lambda bi, hp: (0, hp, 0, 0))
    t4 = t // 4
    f32_slab = pltpu.VMEM((t, LANES), F32)
    scratch = [f32_slab, f32_slab, f32_slab,
               pltpu.VMEM((t + WIN, LANES), BF16), pltpu.VMEM((t + WIN, 2 * LANES), BF16),
               pltpu.VMEM((4, t4 + WIN, LANES), BF16), pltpu.VMEM((4, t4 + WIN, 2 * LANES), BF16)] + [f32_slab] * 6
    return pl.pallas_call(
        functools.partial(_attn_kernel, t=t),
        grid=(b, npairs),
        in_specs=[slab, slab, slab, bias_spec, bias_spec, bias_spec],
        out_specs=slab,
        out_shape=jax.ShapeDtypeStruct((b, npairs, t, LANES), F32),
        scratch_shapes=scratch,
        compiler_params=_params(("arbitrary", "arbitrary"), VMEM_LIMIT),
        name="dilated_attn",
    )(q, k, v, *biases)


def _sample_table(wb, tn, kpad):
    i = np.arange(tn)[:, None]
    c = np.arange(wb + kpad)[None, :]
    dist = wb + i - c
    mult = np.zeros(dist.shape)
    for window, dil in DILATED_BRANCHES:
        mult += (dist >= 0) & (dist <= window) & (dist % dil == 0) & (c < wb + tn)
    logm = np.where(mult > 0, np.log2(np.maximum(mult, 1)), NEG)
    slopes = _alibi_slopes()[:, None, None]
    tab = np.where(mult[None] > 0, -slopes * dist[None] * LOG2E + logm[None], NEG)
    return tab.reshape(N_HEADS * tn, wb + kpad).astype(np.float32)


def _sample_attn_heads(q_ref, kn_ref, vn_ref, kt_ref, vt_ref, tab_ref, o_ref, *, tn):
    nh, _, wb = kt_ref.shape
    kpad = tab_ref.shape[1] - wb
    width = nh * HEAD_DIM
    rows = nh * tn
    row_head = lax.shift_right_logical(lax.broadcasted_iota(jnp.int32, (rows, width), 0), int(math.log2(tn)))
    lane_head = lax.shift_right_logical(lax.broadcasted_iota(jnp.int32, (rows, width), 1), int(math.log2(HEAD_DIM)))
    diag = row_head == lane_head
    qt = jnp.concatenate([q_ref[...].astype(F32)] * nh, axis=0)
    qbd = jnp.where(diag, qt, 0.0).astype(BF16)
    pad = jnp.zeros((kpad - tn, width), F32)
    k_new = jnp.concatenate([kn_ref[...], pad], axis=0).astype(BF16)
    v_new = jnp.concatenate([vn_ref[...], pad], axis=0).astype(BF16)
    kt = kt_ref[...].reshape(width, wb).astype(BF16)
    vt = vt_ref[...].reshape(width, wb).astype(BF16)
    s = jnp.concatenate([_dot(qbd, kt), _dot_nt(qbd, k_new)], axis=1) + tab_ref[...]
    m = jnp.max(s, axis=1, keepdims=True)
    p = jnp.exp2(s - m).astype(BF16)
    den = jnp.sum(p.astype(F32), axis=1, keepdims=True)
    full = (_dot_nt(p[:, :wb], vt) + _dot(p[:, wb:], v_new)) / den
    full = jnp.where(diag, full, 0.0)
    acc = full[0:tn]
    for h in range(1, nh):
        acc = acc + full[h * tn:(h + 1) * tn]
    o_ref[...] = acc


def _expand_matrix(width):
    e = np.zeros((LANES, N_HEADS * width), np.float32)
    for h in range(N_HEADS):
        e[h, h * width:(h + 1) * width] = 1.0
    return e


def _ssd_kernel(*refs, l_in, n_chunks, group, rider_tn):
    (xbc_ref, dt_ref, z_ref, h0_ref, pre_ref, convw_ref, convb_ref, dtb_ref, a_ref, dskip_ref,
     gn_ref, tri_ref, e128_ref, e64_ref) = refs[:14]
    if rider_tn:
        rider_in = refs[14:20]
        y_ref, hout_ref, rider_out, xpad_sc, tail_sc, h_sc = refs[20:]
        _sample_attn_heads(*rider_in, rider_out, tn=rider_tn)
    else:
        y_ref, hout_ref, xpad_sc, tail_sc, h_sc = refs[14:]
    c = pl.program_id(1)

    @pl.when(c == 0)
    def _():
        h_sc[...] = h0_ref[...]
        if l_in < CHUNK:
            xpad_sc[...] = jnp.zeros(xpad_sc.shape, F32)
        for g in range(group):
            for s in range(D_CONV // LANES):
                tail_sc[g, s] = pre_ref[g, :, s * LANES:(s + 1) * LANES]

    for g in range(group):
        _ssd_chunk(xbc_ref.at[g], dt_ref.at[g], z_ref.at[g], convw_ref, convb_ref, dtb_ref, a_ref, dskip_ref,
                   gn_ref, tri_ref, e128_ref, e64_ref, y_ref.at[g], xpad_sc.at[g], tail_sc.at[g], h_sc.at[g],
                   l_in=l_in, n_chunks=n_chunks)

    @pl.when(c == n_chunks - 1)
    def _():
        hout_ref[...] = h_sc[...]


def _ssd_chunk(xbc_ref, dt_ref, z_ref, convw_ref, convb_ref, dtb_ref, a_ref, dskip_ref, gn_ref, tri_ref, e128_ref,
               e64_ref, y_ref, xpad_sc, tail_sc, h_sc, *, l_in, n_chunks):
    n_slabs = D_CONV // LANES

    def time_rows(tau, n):
        return pl.ds(CONV_PITCH * (tau + CONV_PRE), n, stride=CONV_PITCH)

    slabs = []
    for s in range(n_slabs):
        lanes = slice(s * LANES, (s + 1) * LANES)
        xpad_sc[s, time_rows(-CONV_PRE, CONV_PRE), :] = tail_sc[s]
        xpad_sc[s, time_rows(0, l_in), :] = xbc_ref[:, lanes]
        acc = convb_ref[:, lanes]
        for w in range(CONV_WIDTH):
            acc = acc + convw_ref[w:w + 1, lanes] * xpad_sc[s, time_rows(w - (CONV_WIDTH - 1), CHUNK), :]
        slabs.append(_silu(acc))
        if n_chunks > 1:
            tail_sc[s] = xbc_ref[CHUNK - CONV_PRE:, lanes]
    n_x = D_SSM // LANES
    n_bc = N_GROUPS * SSM_STATE // LANES
    xs = jnp.concatenate(slabs[:n_x], axis=1)
    bmat = jnp.concatenate(slabs[n_x:n_x + n_bc], axis=1).astype(BF16)
    cmat = jnp.concatenate(slabs[n_x + n_bc:], axis=1).astype(BF16)

    draw = dt_ref[...] + dtb_ref[...]
    dt = jnp.maximum(draw, 0.0) + jnp.log1p(jnp.exp(-jnp.abs(draw)))
    if l_in < CHUNK:
        dt = jnp.concatenate([dt, jnp.zeros((CHUNK - l_in, DT_PAD), F32)], axis=0)
    la = dt * a_ref[...]
    tri = tri_ref[...]
    la_hi, la_lo = _split_bf16(la)
    cs = _dot(tri, la_hi) + _dot(tri, la_lo)
    cs_t = cs.T
    cs128 = _dot_split(cs, e128_ref[...])
    cs64 = _dot_split(cs, e64_ref[...])
    dt64 = _dot_split(dt, e64_ref[...])
    xdt = xs * dt64
    decay_in = jnp.exp(cs64)
    cs_end = cs64[CHUNK - 1:CHUNK, :]
    xw = xdt * jnp.exp(cs_end - cs64)

    lane = lax.broadcasted_iota(jnp.int32, (1, LANES), 1)
    head0 = lane < HEAD_DIM
    causal = (lax.broadcasted_iota(jnp.int32, (CHUNK, CHUNK), 0)
              >= lax.broadcasted_iota(jnp.int32, (CHUNK, CHUNK), 1))

    y_parts = []
    for g in range(N_GROUPS):
        gs = slice(g * SSM_STATE, (g + 1) * SSM_STATE)
        xg = slice(g * GROUP_W, (g + 1) * GROUP_W)
        b_g = bmat[:, gs]
        c_g = cmat[:, gs]
        cb = _dot_nt(c_g, b_g)
        h_prev = h_sc[xg, :]
        y_off = _dot_nt(c_g, h_prev.astype(BF16)) * decay_in[:, xg]
        y_diag = []
        for hp in range(HEADS_PER_GROUP // 2):
            gmats = []
            for j in range(2):
                h = g * HEADS_PER_GROUP + 2 * hp + j
                seg = cs128[:, h * LANES:(h + 1) * LANES] - cs_t[h:h + 1, :]
                gmats.append((cb * jnp.exp(jnp.where(causal, seg, NEG))).astype(BF16))
            xp = xdt[:, (g * 2 + hp) * LANES:(g * 2 + hp + 1) * LANES]
            rhs = jnp.concatenate([jnp.where(head0, xp, 0.0), jnp.where(head0, 0.0, xp)], axis=0).astype(BF16)
            y_diag.append(_dot(jnp.concatenate(gmats, axis=1), rhs))
        y_parts.append(jnp.concatenate(y_diag, axis=1) + y_off)
        st = _dot(xw[:, xg].T.astype(BF16), b_g)
        for j in range(HEADS_PER_GROUP):
            h = g * HEADS_PER_GROUP + j
            hs = slice(h * HEAD_DIM, (h + 1) * HEAD_DIM)
            dec = jnp.exp(cs128[CHUNK - 1:CHUNK, h * LANES:(h + 1) * LANES])
            h_sc[hs, :] = h_sc[hs, :] * dec + st[j * HEAD_DIM:(j + 1) * HEAD_DIM, :]

    y = jnp.concatenate(y_parts, axis=1) + dskip_ref[...] * xs
    y = y * _silu(z_ref[...].astype(F32)) if l_in == CHUNK else y[:l_in] * _silu(z_ref[...].astype(F32))
    y_ref[...] = _rms_norm(y, gn_ref[...]).astype(BF16)


def _ssd(xbc, dt, z, h0, pre, conv_w, conv_b, dtb_pad, a_pad, dskip64, gnorm, rider=None):
    b, t, _ = xbc.shape
    l_in = min(CHUNK, t)
    nc = t // l_in
    tri = jnp.asarray(np.tril(np.ones((CHUNK, CHUNK), np.float32)), BF16)
    e128 = jnp.asarray(_expand_matrix(LANES), BF16)
    e64 = jnp.asarray(_expand_matrix(HEAD_DIM), BF16)
    group = math.gcd(b, SSD_GROUP)
    step = lambda w: pl.BlockSpec((group, l_in, w), lambda bi, c: (bi, c, 0))
    per_b = lambda r, w: pl.BlockSpec((group, r, w), lambda bi, c: (bi, 0, 0))
    consts = [conv_w, conv_b, dtb_pad, a_pad, dskip64, gnorm, tri, e128, e64]
    args = [xbc, dt, z, h0, pre] + consts
    in_specs = ([step(D_CONV), step(DT_PAD), step(D_SSM), per_b(D_SSM, SSM_STATE), per_b(CONV_PRE, D_CONV)]
                + [_const_spec(a.shape) for a in consts])
    out_specs = [step(D_SSM), per_b(D_SSM, SSM_STATE)]
    out_shape = [jax.ShapeDtypeStruct((b, t, D_SSM), BF16), jax.ShapeDtypeStruct((b, D_SSM, SSM_STATE), F32)]
    rider_tn = 0
    if rider is not None:
        q, k_new, v_new, cache_kt, cache_vt = rider
        bs, rider_tn, _ = q.shape
        wb = cache_kt.shape[-1]
        steps = (b // group) * nc
        parts = steps // bs
        nh = N_HEADS // parts
        assert steps == parts * bs and nh * parts == N_HEADS and nh % 2 == 0 and rider_tn & (rider_tn - 1) == 0
        tab = jnp.asarray(_sample_table(wb, rider_tn, LANES))
        seq = lambda bi, c: (bi * nc + c) // parts
        part = lambda bi, c: (bi * nc + c) % parts
        new = pl.BlockSpec((None, rider_tn, nh * HEAD_DIM), lambda bi, c: (seq(bi, c), 0, part(bi, c)))
        cache = pl.BlockSpec((None, nh, HEAD_DIM, wb), lambda bi, c: (seq(bi, c), part(bi, c), 0, 0))
        args += [q, k_new, v_new, cache_kt, cache_vt, tab]
        in_specs += [new, new, new, cache, cache,
                     pl.BlockSpec((nh * rider_tn, tab.shape[1]), lambda bi, c: (part(bi, c), 0))]
        out_specs.append(new)
        out_shape.append(jax.ShapeDtypeStruct((bs, rider_tn, D_ATT), F32))
    return pl.pallas_call(
        functools.partial(_ssd_kernel, l_in=l_in, n_chunks=nc, group=group, rider_tn=rider_tn),
        grid=(b // group, nc),
        in_specs=in_specs,
        out_specs=out_specs,
        out_shape=out_shape,
        scratch_shapes=[pltpu.VMEM((group, D_CONV // LANES, CONV_PITCH * (CONV_PRE + CHUNK), LANES), F32),
                        pltpu.VMEM((group, D_CONV // LANES, CONV_PRE, LANES), F32),
                        pltpu.VMEM((group, D_SSM, SSM_STATE), F32)],
        compiler_params=_params(("arbitrary", "arbitrary"), VMEM_LIMIT),
        name="ssd",
    )(*args)


def _mix_ffn_kernel(a_ref, ys_ref, x_ref, ga_ref, wo_ref, g1_ref, b1_ref, wg_ref, wu_ref, wd_ref, g2_ref, b2_ref,
                    out_ref, *, pair_major, alpha):
    if pair_major:
        attn = jnp.concatenate([a_ref[hp] for hp in range(N_PAIRS)], axis=1)
    else:
        attn = a_ref[...]
    a = _rms_norm(attn, ga_ref[...]).astype(BF16)
    mix = _dot(a, wo_ref[0:D_ATT, :]) + _dot(ys_ref[...], wo_ref[D_ATT:, :])
    h = _layer_norm(alpha * x_ref[...] + mix, g1_ref[...], b1_ref[...])
    hb = h.astype(BF16)
    acc = None
    for c in range(D_FF // FF_CHUNK):
        cols = slice(c * FF_CHUNK, (c + 1) * FF_CHUNK)
        act = (_silu(_dot(hb, wg_ref[:, cols])) * _dot(hb, wu_ref[:, cols])).astype(BF16)
        part = _dot(act, wd_ref[cols, :])
        acc = part if acc is None else acc + part
    out_ref[...] = _layer_norm(alpha * h + acc, g2_ref[...], b2_ref[...])


def _mix_ffn(attn, y_ssm, x, w, tm, alpha, pair_major):
    b, t, _ = x.shape
    row = lambda wd: pl.BlockSpec((None, tm, wd), lambda bi, i: (bi, i, 0))
    a_spec = pl.BlockSpec((None, N_PAIRS, tm, LANES), lambda bi, i: (bi, 0, i, 0)) if pair_major else row(D_ATT)
    single = lambda arr: pl.BlockSpec(arr.shape, lambda bi, i: (0, 0), pipeline_mode=pl.Buffered(1))
    vec = lambda arr: _const_spec(arr.shape)
    return pl.pallas_call(
        functools.partial(_mix_ffn_kernel, pair_major=pair_major, alpha=alpha),
        grid=(b, t // tm),
        in_specs=[a_spec, row(D_SSM), row(D_MODEL), vec(w["g_attn"]), single(w["w_out"]), vec(w["ln1_g"]),
                  vec(w["ln1_b"]), single(w["w_gate"]), single(w["w_up"]), single(w["w_down"]), vec(w["ln2_g"]),
                  vec(w["ln2_b"])],
        out_specs=row(D_MODEL),
        out_shape=jax.ShapeDtypeStruct((b, t, D_MODEL), F32),
        compiler_params=_params(("arbitrary", "arbitrary"), VMEM_LIMIT),
        name="mix_ffn",
    )(attn, y_ssm, x, w["g_attn"], w["w_out"], w["ln1_g"], w["ln1_b"], w["w_gate"], w["w_up"], w["w_down"],
      w["ln2_g"], w["ln2_b"])


def _pad_lanes(v, width=LANES):
    return jnp.pad(v.astype(F32), (0, width - v.shape[0])).reshape(1, width)


def _layer_weights(w_in, conv_w, conv_b, dt_bias, a_log, d_skip, attn_norm_g, ssm_norm_g, w_out, ln1_g, ln1_b,
                   w_gate, w_up, w_down, ln2_g, ln2_b):
    row = lambda v: v.astype(F32).reshape(1, -1)
    return dict(
        w_in=jnp.concatenate([w_in.astype(BF16), jnp.zeros((D_MODEL, D_PROJ_PAD - w_in.shape[1]), BF16)], axis=1),
        conv_w=conv_w.astype(F32), conv_b=row(conv_b),
        dtb=_pad_lanes(dt_bias), a=_pad_lanes(-jnp.exp(a_log.astype(F32))),
        dskip=row(jnp.repeat(d_skip.astype(F32), HEAD_DIM)),
        g_attn=row(attn_norm_g), g_ssm=row(ssm_norm_g), w_out=w_out.astype(BF16),
        ln1_g=row(ln1_g), ln1_b=row(ln1_b), w_gate=w_gate.astype(BF16), w_up=w_up.astype(BF16),
        w_down=w_down.astype(BF16), ln2_g=row(ln2_g), ln2_b=row(ln2_b))


def _ssd_weights(w):
    return w["conv_w"], w["conv_b"], w["dtb"], w["a"], w["dskip"], w["g_ssm"]


def _layer(xp, xs, cache_k, cache_v, state_ssm, state_conv, w, alpha):
    bp, t, _ = xp.shape
    bs, tn, _ = xs.shape
    keep = min(MAX_WINDOW, t)
    rows = bs * tn
    flat = lambda a: a.reshape(1, rows, a.shape[-1])
    per_b = lambda a: a.reshape(bs, tn, a.shape[-1])
    heads = lambda a: a.reshape(a.shape[0], a.shape[1], N_HEADS, HEAD_DIM)
    state = lambda h: h.reshape(h.shape[0], N_HEADS, HEAD_DIM, SSM_STATE)

    q_s, kf_s, vf_s, z_s, xbc_s, dt_s = _in_proj(flat(xs), w["w_in"], rows, rows, False)
    kf_s, vf_s, xbc_s = per_b(kf_s), per_b(vf_s), per_b(xbc_s)
    q, kp, vp, kf, vf, z, xbc, dt = _in_proj(xp, w["w_in"], ROW_TILE, keep, True)
    attn = _attn(q, kp, vp)

    cache_t = lambda c: jnp.transpose(c, (0, 2, 3, 1))
    rider = (per_b(q_s), kf_s, vf_s, cache_t(cache_k), cache_t(cache_v))
    h0 = jnp.zeros((bp, D_SSM, SSM_STATE), F32)
    pre = jnp.zeros((bp, CONV_PRE, D_CONV), F32)
    y_ssm, h_fin, attn_s = _ssd(xbc, dt, z, h0, pre, *_ssd_weights(w), rider=rider)
    yp = _mix_ffn(attn, y_ssm, xp, w, ROW_TILE, alpha, True)
    conv_new = xbc[:, t - (CONV_WIDTH - 1):, :]

    pre_s = jnp.pad(state_conv.astype(F32), ((0, 0), (CONV_PRE - (CONV_WIDTH - 1), 0), (0, 0)))
    y_ssm_s, h_fin_s = _ssd(xbc_s, per_b(dt_s), per_b(z_s), state_ssm.reshape(bs, D_SSM, SSM_STATE).astype(F32),
                            pre_s, *_ssd_weights(w))
    ys = _mix_ffn(flat(attn_s), flat(y_ssm_s), flat(xs), w, rows, alpha, False)
    conv_new_s = jnp.concatenate([state_conv.astype(F32), xbc_s], axis=1)[:, -(CONV_WIDTH - 1):, :]
    return (yp, per_b(ys[0]), heads(kf), heads(vf), state(h_fin), conv_new,
            heads(kf_s), heads(vf_s), state(h_fin_s), conv_new_s)


def kernel(x_prompt, x_sample, cache_k_win, cache_v_win, state_ssm, state_conv, w_in, conv_w, conv_b, dt_bias, a_log, d_skip, attn_norm_g, ssm_norm_g, w_out, ln1_g, ln1_b, w_gate, w_up, w_down, ln2_g, ln2_b):
    depth = w_in.shape[0]
    alpha = (2.0 * depth) ** 0.25
    hp, hs = x_prompt, x_sample
    cols = [[] for _ in range(8)]
    for l in range(depth):
        w = _layer_weights(w_in[l], conv_w[l], conv_b[l], dt_bias[l], a_log[l], d_skip[l], attn_norm_g[l],
                           ssm_norm_g[l], w_out[l], ln1_g[l], ln1_b[l], w_gate[l], w_up[l], w_down[l],
                           ln2_g[l], ln2_b[l])
        hp, hs, *layer_state = _layer(hp, hs, cache_k_win[l], cache_v_win[l], state_ssm[l], state_conv[l], w, alpha)
        for dst, val in zip(cols, layer_state):
            dst.append(val)
    return (hp, hs) + tuple(jnp.stack(c) for c in cols)
```

```python
import functools
import math

import numpy as np
import jax
import jax.numpy as jnp
from jax import lax
from jax.experimental import pallas as pl
from jax.experimental.pallas import tpu as pltpu

F32 = jnp.float32
BF16 = jnp.bfloat16

D_MODEL = 1024
HEAD_DIM = 64
N_HEADS = 16
N_PAIRS = N_HEADS // 2
D_ATT = N_HEADS * HEAD_DIM
D_SSM = N_HEADS * HEAD_DIM
SSM_STATE = 128
N_GROUPS = 4
HEADS_PER_GROUP = N_HEADS // N_GROUPS
GROUP_W = HEADS_PER_GROUP * HEAD_DIM
CONV_WIDTH = 4
D_CONV = D_SSM + 2 * N_GROUPS * SSM_STATE
D_FF = 2816
FF_CHUNK = D_FF // 2
CHUNK = 128
CONV_PRE = 8
CONV_PITCH = 2
SSD_GROUP = 2
DILATED_BRANCHES = ((128, 1), (512, 4), (2048, 16))
WIN = 128
MAX_WINDOW = 2048
LN_EPS = 1e-5
RMS_EPS = 1e-5
LANES = 128
DT_PAD = LANES
LOG2E = math.log2(math.e)
Q_SCALE = HEAD_DIM ** -0.5 * LOG2E
NEG = -1e30
VMEM_LIMIT = 56 * 1024 * 1024
ROW_TILE = 512


def _alibi_slopes():
    return 2.0 ** (-8.0 * np.arange(1, N_HEADS + 1) / N_HEADS)


def _params(sem, vmem=None):
    return pltpu.CompilerParams(dimension_semantics=sem, vmem_limit_bytes=vmem)


def _const_spec(shape):
    nd = len(shape)
    return pl.BlockSpec(shape, lambda *_: (0,) * nd)


def _split_bf16(x):
    hi = x.astype(BF16)
    lo = (x - hi.astype(F32)).astype(BF16)
    return hi, lo


def _dot(a, b):
    return jnp.dot(a, b, preferred_element_type=F32)


def _dot_nt(a, b):
    return lax.dot_general(a, b, (((1,), (1,)), ((), ())), preferred_element_type=F32)


def _dot_split(x, e):
    hi, lo = _split_bf16(x)
    return _dot(hi, e) + _dot(lo, e)


def _layer_norm(x, g, b):
    mu = jnp.mean(x, axis=-1, keepdims=True)
    xc = x - mu
    var = jnp.mean(xc * xc, axis=-1, keepdims=True)
    return xc * lax.rsqrt(var + LN_EPS) * g + b


def _rms_norm(x, g):
    return x * lax.rsqrt(jnp.mean(x * x, axis=-1, keepdims=True) + RMS_EPS) * g


def _silu(x):
    return x * (1.0 / (1.0 + jnp.exp(-x)))


def _in_proj_kernel(x_ref, w_ref, *out_refs, pair_major):
    xb = x_ref[...].astype(BF16)

    def seg(lo, hi):
        return _dot_nt(xb, w_ref[lo:hi, :])

    def put_pairs(ref, val):
        for hp in range(N_PAIRS):
            ref[hp] = val[:, hp * LANES:(hp + 1) * LANES]

    q = seg(0, D_ATT) * Q_SCALE
    k = seg(D_ATT, 2 * D_ATT)
    v = seg(2 * D_ATT, 3 * D_ATT)
    if pair_major:
        q_ref, kp_ref, vp_ref, kf_ref, vf_ref, z_ref, xbc_ref, dt_ref = out_refs
        put_pairs(q_ref, q)
        put_pairs(kp_ref, k)
        put_pairs(vp_ref, v)
    else:
        q_ref, kf_ref, vf_ref, z_ref, xbc_ref, dt_ref = out_refs
        q_ref[...] = q.astype(BF16)
    kf_ref[...] = k
    vf_ref[...] = v
    o = 3 * D_ATT
    z_ref[...] = seg(o, o + D_SSM).astype(BF16)
    o += D_SSM
    xbc_ref[...] = seg(o, o + D_CONV)
    o += D_CONV
    n_dt = w_ref.shape[0] - o
    w_dt = jnp.concatenate([w_ref[o:, :], jnp.zeros((DT_PAD - n_dt, D_MODEL), BF16)], axis=0)
    dt_ref[...] = _dot_nt(xb, w_dt)


def _in_proj(x, w_t, tm, keep, pair_major):
    b, t, _ = x.shape
    nt = t // tm
    skip = (t - keep) // tm
    row = lambda bi, i: (bi, i, 0)
    kept = lambda bi, i: (bi, jnp.maximum(i - skip, 0), 0)

    def spec(w, imap=row):
        return pl.BlockSpec((None, tm, w), imap)

    sds = jax.ShapeDtypeStruct
    pair_spec = pl.BlockSpec((None, N_PAIRS, tm, LANES), lambda bi, i: (bi, 0, i, 0))
    pair_shape = sds((b, N_PAIRS, t, LANES), F32)
    tail_specs = [spec(D_ATT, kept), spec(D_ATT, kept), spec(D_SSM), spec(D_CONV), spec(DT_PAD)]
    tail_shapes = [sds((b, keep, D_ATT), F32), sds((b, keep, D_ATT), F32),
                   sds((b, t, D_SSM), BF16), sds((b, t, D_CONV), F32), sds((b, t, DT_PAD), F32)]
    if pair_major:
        out_specs = [pair_spec] * 3 + tail_specs
        out_shape = [pair_shape] * 3 + tail_shapes
    else:
        out_specs = [spec(D_ATT)] + tail_specs
        out_shape = [sds((b, t, D_ATT), BF16)] + tail_shapes
    return pl.pallas_call(
        functools.partial(_in_proj_kernel, pair_major=pair_major),
        grid=(b, nt),
        in_specs=[spec(D_MODEL),
                  pl.BlockSpec(w_t.shape, lambda bi, i: (0, 0), pipeline_mode=pl.Buffered(1))],
        out_specs=out_specs,
        out_shape=out_shape,
        compiler_params=_params(("arbitrary", "arbitrary"), VMEM_LIMIT),
        name="in_proj",
    )(x, w_t)


def _branch_bias(dil):
    r = np.arange(WIN)[:, None]
    c = np.arange(2 * WIN)[None, :]
    ju = WIN + r - c
    valid = (ju >= 0) & (ju <= WIN)
    slopes = _alibi_slopes()[:, None, None]
    base = -slopes * (ju * dil)[None] * LOG2E
    full = np.where(valid[None], base, NEG)
    first = np.where((valid & (c >= WIN))[None], base, NEG)
    return np.stack([full, first]).astype(np.float32)


def _attn_block(qf, kblk, vblk, bias_ref, var, head0):
    q2 = jnp.concatenate([jnp.where(head0, qf, 0.0), jnp.where(head0, 0.0, qf)], axis=0).astype(BF16)
    s = _dot_nt(q2, kblk) + jnp.concatenate([bias_ref[var, 0], bias_ref[var, 1]], axis=0)
    m = jnp.max(s, axis=1, keepdims=True)
    p = jnp.exp2(s - m).astype(BF16)
    r0 = _dot(p[:WIN], vblk)
    r1 = _dot(p[WIN:], vblk)
    den = jnp.where(head0, r0[:, LANES:], r1[:, LANES:])
    num = jnp.where(head0, r0[:, :LANES], r1[:, :LANES])
    lse = jnp.where(head0, m[:WIN], m[WIN:]) + jnp.log2(den)
    return num / den, lse


def _attn_kernel(q_ref, k_ref, v_ref, b1_ref, b4_ref, b16_ref, o_ref,
                 q4_sc, k4_sc, v4_sc, kb1_sc, va1_sc, kb4_sc, va4_sc,
                 o1_sc, l1_sc, o16_sc, l16_sc, *, t):
    t4 = t // 4
    t16 = t // 16
    n1 = t // WIN
    n4 = t4 // WIN
    lane = lax.broadcasted_iota(jnp.int32, (1, LANES), 1)
    head0 = lane < HEAD_DIM
    zeros_k = jnp.zeros((WIN, LANES), BF16)
    zeros_v = jnp.zeros((WIN, 2 * LANES), BF16)

    def rows(start, n=WIN):
        return pl.ds(start, n)

    def with_ones(vf):
        return jnp.concatenate([vf.astype(BF16), jnp.ones(vf.shape, BF16)], axis=1)


    kb1_sc[0:WIN, :] = zeros_k
    va1_sc[0:WIN, :] = zeros_v
    for c in range(4):
        kb4_sc[c, 0:WIN, :] = zeros_k
        va4_sc[c, 0:WIN, :] = zeros_v

    for j in range(n1):
        src = rows(j * WIN)
        dst = rows(WIN + j * WIN)
        kb1_sc[dst, :] = k_ref[src, :].astype(BF16)
        va1_sc[dst, :] = with_ones(v_ref[src, :])

    for c in range(4):
        for j in range(n4):
            src = pl.ds(c + 4 * WIN * j, WIN, stride=4)
            dst = rows(c * t4 + j * WIN)
            kc = k_ref[src, :]
            vc = v_ref[src, :]
            k4_sc[dst, :] = kc
            v4_sc[dst, :] = vc
            q4_sc[dst, :] = q_ref[src, :]
            kb4_sc[c, rows(WIN + j * WIN), :] = kc.astype(BF16)
            va4_sc[c, rows(WIN + j * WIN), :] = with_ones(vc)

    for j in range(n1):
        out, lse = _attn_block(q_ref[rows(j * WIN), :], kb1_sc[rows(j * WIN, 2 * WIN), :],
                               va1_sc[rows(j * WIN, 2 * WIN), :], b1_ref, int(j == 0), head0)
        o1_sc[rows(j * WIN), :] = out
        l1_sc[rows(j * WIN), :] = lse

    for cls in range(16):
        r4, c = divmod(cls, 4)
        base = r4 * t4 + c
        src = pl.ds(base, t16, stride=4)
        kfull = jnp.concatenate([zeros_k, k4_sc[src, :].astype(BF16)], axis=0)
        vfull = jnp.concatenate([zeros_v, with_ones(v4_sc[src, :])], axis=0)
        for j in range(t16 // WIN):
            dst = pl.ds(base + 4 * WIN * j, WIN, stride=4)
            out, lse = _attn_block(q4_sc[dst, :], kfull[j * WIN:(j + 2) * WIN], vfull[j * WIN:(j + 2) * WIN],
                                   b16_ref, int(j == 0), head0)
            o16_sc[dst, :] = out
            l16_sc[dst, :] = lse

    for c in range(4):
        for j in range(n4):
            cm = rows(c * t4 + j * WIN)
            nat = pl.ds(c + 4 * WIN * j, WIN, stride=4)
            o4, l4 = _attn_block(q4_sc[cm, :], kb4_sc[c, rows(j * WIN, 2 * WIN), :],
                                 va4_sc[c, rows(j * WIN, 2 * WIN), :], b4_ref, int(j == 0), head0)
            l1 = l1_sc[nat, :]
            l16 = l16_sc[cm, :]
            m = jnp.maximum(jnp.maximum(l1, l4), l16)
            e1 = jnp.exp2(l1 - m)
            e4 = jnp.exp2(l4 - m)
            e16 = jnp.exp2(l16 - m)
            acc = e1 * o1_sc[nat, :] + e4 * o4 + e16 * o16_sc[cm, :]
            o_ref[nat, :] = acc / (e1 + e4 + e16)


def _attn(q, k, v):
    b, npairs, t, _ = q.shape
    assert t % (16 * WIN) == 0
    slab = pl.BlockSpec((None, None, t, LANES), lambda bi, hp: (bi, hp, 0, 0))
    biases = [jnp.asarray(_branch_bias(dil)) for _, dil in DILATED_BRANCHES]
    bias_spec = pl.BlockSpec((2, 2, WIN, 2 * WIN), lambda bi, hp: (0, hp, 0, 0))
    t4 = t // 4
    f32_slab = pltpu.VMEM((t, LANES), F32)
    scratch = [f32_slab, f32_slab, f32_slab,
               pltpu.VMEM((t + WIN, LANES), BF16), pltpu.VMEM((t + WIN, 2 * LANES), BF16),
               pltpu.VMEM((4, t4 + WIN, LANES), BF16), pltpu.VMEM((4, t4 + WIN, 2 * LANES), BF16)] + [f32_slab] * 4
    return pl.pallas_call(
        functools.partial(_attn_kernel, t=t),
        grid=(b, npairs),
        in_specs=[slab, slab, slab, bias_spec, bias_spec, bias_spec],
        out_specs=slab,
        out_shape=jax.ShapeDtypeStruct((b, npairs, t, LANES), F32),
        scratch_shapes=scratch,
        compiler_params=_params(("arbitrary", "arbitrary"), VMEM_LIMIT),
        name="dilated_attn",
    )(q, k, v, *biases)


def _sample_table(wb, tn, kpad):
    i = np.arange(tn)[:, None]
    c = np.arange(wb + kpad)[None, :]
    dist = wb + i - c
    mult = np.zeros(dist.shape)
    for window, dil in DILATED_BRANCHES:
        mult += (dist >= 0) & (dist <= window) & (dist % dil == 0) & (c < wb + tn)
    logm = np.where(mult > 0, np.log2(np.maximum(mult, 1)), NEG)
    slopes = _alibi_slopes()[:, None, None]
    tab = np.where(mult[None] > 0, -slopes * dist[None] * LOG2E + logm[None], NEG)
    return tab.reshape(N_HEADS * tn, wb + kpad).astype(np.float32)


def _sample_attn_heads(q_ref, kn_ref, vn_ref, kt_ref, vt_ref, tab_ref, o_ref, *, tn):
    nh, _, wb = kt_ref.shape
    kpad = tab_ref.shape[1] - wb
    width = nh * HEAD_DIM
    rows = nh * tn
    row_head = lax.shift_right_logical(lax.broadcasted_iota(jnp.int32, (rows, width), 0), int(math.log2(tn)))
    lane_head = lax.shift_right_logical(lax.broadcasted_iota(jnp.int32, (rows, width), 1), int(math.log2(HEAD_DIM)))
    diag = row_head == lane_head
    qt = jnp.concatenate([q_ref[...].astype(F32)] * nh, axis=0)
    qbd = jnp.where(diag, qt, 0.0).astype(BF16)
    pad = jnp.zeros((kpad - tn, width), F32)
    k_new = jnp.concatenate([kn_ref[...], pad], axis=0).astype(BF16)
    v_new = jnp.concatenate([vn_ref[...], pad], axis=0).astype(BF16)
    kt = kt_ref[...].reshape(width, wb).astype(BF16)
    vt = vt_ref[...].reshape(width, wb).astype(BF16)
    s = jnp.concatenate([_dot(qbd, kt), _dot_nt(qbd, k_new)], axis=1) + tab_ref[...]
    m = jnp.max(s, axis=1, keepdims=True)
    p = jnp.exp2(s - m).astype(BF16)
    den = jnp.sum(p.astype(F32), axis=1, keepdims=True)
    full = (_dot_nt(p[:, :wb], vt) + _dot(p[:, wb:], v_new)) / den
    full = jnp.where(diag, full, 0.0)
    acc = full[0:tn]
    for h in range(1, nh):
        acc = acc + full[h * tn:(h + 1) * tn]
    o_ref[...] = acc


def _expand_matrix(width):
    e = np.zeros((LANES, N_HEADS * width), np.float32)
    for h in range(N_HEADS):
        e[h, h * width:(h + 1) * width] = 1.0
    return e


def _ssd_kernel(*refs, l_in, n_chunks, group, rider_tn):
    (xbc_ref, dt_ref, z_ref, h0_ref, pre_ref, convw_ref, convb_ref, dtb_ref, a_ref, dskip_ref,
     gn_ref, tri_ref, e128_ref, e64_ref) = refs[:14]
    if rider_tn:
        rider_in = refs[14:20]
        y_ref, hout_ref, rider_out, xpad_sc, tail_sc, h_sc = refs[20:]
        _sample_attn_heads(*rider_in, rider_out, tn=rider_tn)
    else:
        y_ref, hout_ref, xpad_sc, tail_sc, h_sc = refs[14:]
    c = pl.program_id(1)

    @pl.when(c == 0)
    def _():
        h_sc[...] = h0_ref[...]
        if l_in < CHUNK:
            xpad_sc[...] = jnp.zeros(xpad_sc.shape, F32)
        for g in range(group):
            for s in range(D_CONV // LANES):
                tail_sc[g, s] = pre_ref[g, :, s * LANES:(s + 1) * LANES]

    for g in range(group):
        _ssd_chunk(xbc_ref.at[g], dt_ref.at[g], z_ref.at[g], convw_ref, convb_ref, dtb_ref, a_ref, dskip_ref,
                   gn_ref, tri_ref, e128_ref, e64_ref, y_ref.at[g], xpad_sc.at[g], tail_sc.at[g], h_sc.at[g],
                   l_in=l_in, n_chunks=n_chunks)

    @pl.when(c == n_chunks - 1)
    def _():
        hout_ref[...] = h_sc[...]


def _ssd_chunk(xbc_ref, dt_ref, z_ref, convw_ref, convb_ref, dtb_ref, a_ref, dskip_ref, gn_ref, tri_ref, e128_ref,
               e64_ref, y_ref, xpad_sc, tail_sc, h_sc, *, l_in, n_chunks):
    n_slabs = D_CONV // LANES

    def time_rows(tau, n):
        return pl.ds(CONV_PITCH * (tau + CONV_PRE), n, stride=CONV_PITCH)

    slabs = []
    for s in range(n_slabs):
        lanes = slice(s * LANES, (s + 1) * LANES)
        xpad_sc[s, time_rows(-CONV_PRE, CONV_PRE), :] = tail_sc[s]
        xpad_sc[s, time_rows(0, l_in), :] = xbc_ref[:, lanes]
        acc = convb_ref[:, lanes]
        for w in range(CONV_WIDTH):
            acc = acc + convw_ref[w:w + 1, lanes] * xpad_sc[s, time_rows(w - (CONV_WIDTH - 1), CHUNK), :]
        slabs.append(_silu(acc))
        if n_chunks > 1:
            tail_sc[s] = xbc_ref[CHUNK - CONV_PRE:, lanes]
    n_x = D_SSM // LANES
    n_bc = N_GROUPS * SSM_STATE // LANES
    xs = jnp.concatenate(slabs[:n_x], axis=1)
    bmat = jnp.concatenate(slabs[n_x:n_x + n_bc], axis=1).astype(BF16)
    cmat = jnp.concatenate(slabs[n_x + n_bc:], axis=1).astype(BF16)

    draw = dt_ref[...] + dtb_ref[...]
    dt = jnp.maximum(draw, 0.0) + jnp.log1p(jnp.exp(-jnp.abs(draw)))
    if l_in < CHUNK:
        dt = jnp.concatenate([dt, jnp.zeros((CHUNK - l_in, DT_PAD), F32)], axis=0)
    la = dt * a_ref[...]
    tri = tri_ref[...]
    la_hi, la_lo = _split_bf16(la)
    cs = _dot(tri, la_hi) + _dot(tri, la_lo)
    cs_t = cs.T
    cs128 = _dot_split(cs, e128_ref[...])
    cs64 = _dot_split(cs, e64_ref[...])
    dt64 = _dot_split(dt, e64_ref[...])
    xdt = xs * dt64
    decay_in = jnp.exp(cs64)
    cs_end = cs64[CHUNK - 1:CHUNK, :]
    xw = xdt * jnp.exp(cs_end - cs64)

    lane = lax.broadcasted_iota(jnp.int32, (1, LANES), 1)
    head0 = lane < HEAD_DIM
    causal = (lax.broadcasted_iota(jnp.int32, (CHUNK, CHUNK), 0)
              >= lax.broadcasted_iota(jnp.int32, (CHUNK, CHUNK), 1))

    y_parts = []
    for g in range(N_GROUPS):
        gs = slice(g * SSM_STATE, (g + 1) * SSM_STATE)
        xg = slice(g * GROUP_W, (g + 1) * GROUP_W)
        b_g = bmat[:, gs]
        c_g = cmat[:, gs]
        cb = _dot_nt(c_g, b_g)
        h_prev = h_sc[xg, :]
        y_off = _dot_nt(c_g, h_prev.astype(BF16)) * decay_in[:, xg]
        y_diag = []
        for hp in range(HEADS_PER_GROUP // 2):
            gmats = []
            for j in range(2):
                h = g * HEADS_PER_GROUP + 2 * hp + j
                seg = cs128[:, h * LANES:(h + 1) * LANES] - cs_t[h:h + 1, :]
                gmats.append((cb * jnp.exp(jnp.where(causal, seg, NEG))).astype(BF16))
            xp = xdt[:, (g * 2 + hp) * LANES:(g * 2 + hp + 1) * LANES]
            rhs = jnp.concatenate([jnp.where(head0, xp, 0.0), jnp.where(head0, 0.0, xp)], axis=0).astype(BF16)
            y_diag.append(_dot(jnp.concatenate(gmats, axis=1), rhs))
        y_parts.append(jnp.concatenate(y_diag, axis=1) + y_off)
        st = _dot(xw[:, xg].T.astype(BF16), b_g)
        for j in range(HEADS_PER_GROUP):
            h = g * HEADS_PER_GROUP + j
            hs = slice(h * HEAD_DIM, (h + 1) * HEAD_DIM)
            dec = jnp.exp(cs128[CHUNK - 1:CHUNK, h * LANES:(h + 1) * LANES])
            h_sc[hs, :] = h_sc[hs, :] * dec + st[j * HEAD_DIM:(j + 1) * HEAD_DIM, :]

    y = jnp.concatenate(y_parts, axis=1) + dskip_ref[...] * xs
    y = y * _silu(z_ref[...].astype(F32)) if l_in == CHUNK else y[:l_in] * _silu(z_ref[...].astype(F32))
    y_ref[...] = _rms_norm(y, gn_ref[...]).astype(BF16)


def _ssd(xbc, dt, z, h0, pre, conv_w, conv_b, dtb_pad, a_pad, dskip64, gnorm, rider=None):
    b, t, _ = xbc.shape
    l_in = min(CHUNK, t)
    nc = t // l_in
    tri = jnp.asarray(np.tril(np.ones((CHUNK, CHUNK), np.float32)), BF16)
    e128 = jnp.asarray(_expand_matrix(LANES), BF16)
    e64 = jnp.asarray(_expand_matrix(HEAD_DIM), BF16)
    group = math.gcd(b, SSD_GROUP)
    step = lambda w: pl.BlockSpec((group, l_in, w), lambda bi, c: (bi, c, 0))
    per_b = lambda r, w: pl.BlockSpec((group, r, w), lambda bi, c: (bi, 0, 0))
    consts = [conv_w, conv_b, dtb_pad, a_pad, dskip64, gnorm, tri, e128, e64]
    args = [xbc, dt, z, h0, pre] + consts
    in_specs = ([step(D_CONV), step(DT_PAD), step(D_SSM), per_b(D_SSM, SSM_STATE), per_b(CONV_PRE, D_CONV)]
                + [_const_spec(a.shape) for a in consts])
    out_specs = [step(D_SSM), per_b(D_SSM, SSM_STATE)]
    out_shape = [jax.ShapeDtypeStruct((b, t, D_SSM), BF16), jax.ShapeDtypeStruct((b, D_SSM, SSM_STATE), F32)]
    rider_tn = 0
    if rider is not None:
        q, k_new, v_new, cache_kt, cache_vt = rider
        bs, rider_tn, _ = q.shape
        wb = cache_kt.shape[-1]
        steps = (b // group) * nc
        parts = steps // bs
        nh = N_HEADS // parts
        assert steps == parts * bs and nh * parts == N_HEADS and nh % 2 == 0 and rider_tn & (rider_tn - 1) == 0
        tab = jnp.asarray(_sample_table(wb, rider_tn, LANES))
        seq = lambda bi, c: (bi * nc + c) // parts
        part = lambda bi, c: (bi * nc + c) % parts
        new = pl.BlockSpec((None, rider_tn, nh * HEAD_DIM), lambda bi, c: (seq(bi, c), 0, part(bi, c)))
        cache = pl.BlockSpec((None, nh, HEAD_DIM, wb), lambda bi, c: (seq(bi, c), part(bi, c), 0, 0))
        args += [q, k_new, v_new, cache_kt, cache_vt, tab]
        in_specs += [new, new, new, cache, cache,
                     pl.BlockSpec((nh * rider_tn, tab.shape[1]), lambda bi, c: (part(bi, c), 0))]
        out_specs.append(new)
        out_shape.append(jax.ShapeDtypeStruct((bs, rider_tn, D_ATT), F32))
    return pl.pallas_call(
        functools.partial(_ssd_kernel, l_in=l_in, n_chunks=nc, group=group, rider_tn=rider_tn),
        grid=(b // group, nc),
        in_specs=in_specs,
        out_specs=out_specs,
        out_shape=out_shape,
        scratch_shapes=[pltpu.VMEM((group, D_CONV // LANES, CONV_PITCH * (CONV_PRE + CHUNK), LANES), F32),
                        pltpu.VMEM((group, D_CONV // LANES, CONV_PRE, LANES), F32),
                        pltpu.VMEM((group, D_SSM, SSM_STATE), F32)],
        compiler_params=_params(("arbitrary", "arbitrary"), VMEM_LIMIT),
        name="ssd",
    )(*args)


def _mix_ffn_kernel(a_ref, ys_ref, x_ref, ga_ref, wo_ref, g1_ref, b1_ref, wg_ref, wu_ref, wd_ref, g2_ref, b2_ref,
                    out_ref, *, pair_major, alpha):
    if pair_major:
        attn = jnp.concatenate([a_ref[hp] for hp in range(N_PAIRS)], axis=1)
    else:
        attn = a_ref[...]
    a = _rms_norm(attn, ga_ref[...]).astype(BF16)
    mix = _dot(a, wo_ref[0:D_ATT, :]) + _dot(ys_ref[...], wo_ref[D_ATT:, :])
    h = _layer_norm(alpha * x_ref[...] + mix, g1_ref[...], b1_ref[...])
    hb = h.astype(BF16)
    acc = None
    for c in range(D_FF // FF_CHUNK):
        cols = slice(c * FF_CHUNK, (c + 1) * FF_CHUNK)
        act = (_silu(_dot(hb, wg_ref[:, cols])) * _dot(hb, wu_ref[:, cols])).astype(BF16)
        part = _dot(act, wd_ref[cols, :])
        acc = part if acc is None else acc + part
    out_ref[...] = _layer_norm(alpha * h + acc, g2_ref[...], b2_ref[...])


def _mix_ffn(attn, y_ssm, x, w, tm, alpha, pair_major):
    b, t, _ = x.shape
    row = lambda wd: pl.BlockSpec((None, tm, wd), lambda bi, i: (bi, i, 0))
    a_spec = pl.BlockSpec((None, N_PAIRS, tm, LANES), lambda bi, i: (bi, 0, i, 0)) if pair_major else row(D_ATT)
    single = lambda arr: pl.BlockSpec(arr.shape, lambda bi, i: (0, 0), pipeline_mode=pl.Buffered(1))
    vec = lambda arr: _const_spec(arr.shape)
    return pl.pallas_call(
        functools.partial(_mix_ffn_kernel, pair_major=pair_major, alpha=alpha),
        grid=(b, t // tm),
        in_specs=[a_spec, row(D_SSM), row(D_MODEL), vec(w["g_attn"]), single(w["w_out"]), vec(w["ln1_g"]),
                  vec(w["ln1_b"]), single(w["w_gate"]), single(w["w_up"]), single(w["w_down"]), vec(w["ln2_g"]),
                  vec(w["ln2_b"])],
        out_specs=row(D_MODEL),
        out_shape=jax.ShapeDtypeStruct((b, t, D_MODEL), F32),
        compiler_params=_params(("arbitrary", "arbitrary"), VMEM_LIMIT),
        name="mix_ffn",
    )(attn, y_ssm, x, w["g_attn"], w["w_out"], w["ln1_g"], w["ln1_b"], w["w_gate"], w["w_up"], w["w_down"],
      w["ln2_g"], w["ln2_b"])


def _pad_lanes(v, width=LANES):
    return jnp.pad(v.astype(F32), (0, width - v.shape[0])).reshape(1, width)


def _layer_weights(w_in, conv_w, conv_b, dt_bias, a_log, d_skip, attn_norm_g, ssm_norm_g, w_out, ln1_g, ln1_b,
                   w_gate, w_up, w_down, ln2_g, ln2_b):
    row = lambda v: v.astype(F32).reshape(1, -1)
    return dict(
        w_in=w_in.T.astype(BF16),
        conv_w=conv_w.astype(F32), conv_b=row(conv_b),
        dtb=_pad_lanes(dt_bias), a=_pad_lanes(-jnp.exp(a_log.astype(F32))),
        dskip=row(jnp.repeat(d_skip.astype(F32), HEAD_DIM)),
        g_attn=row(attn_norm_g), g_ssm=row(ssm_norm_g), w_out=w_out.astype(BF16),
        ln1_g=row(ln1_g), ln1_b=row(ln1_b), w_gate=w_gate.astype(BF16), w_up=w_up.astype(BF16),
        w_down=w_down.astype(BF16), ln2_g=row(ln2_g), ln2_b=row(ln2_b))


def _ssd_weights(w):
    return w["conv_w"], w["conv_b"], w["dtb"], w["a"], w["dskip"], w["g_ssm"]


def _layer(xp, xs, cache_k, cache_v, state_ssm, state_conv, w, alpha):
    bp, t, _ = xp.shape
    bs, tn, _ = xs.shape
    keep = min(MAX_WINDOW, t)
    rows = bs * tn
    flat = lambda a: a.reshape(1, rows, a.shape[-1])
    per_b = lambda a: a.reshape(bs, tn, a.shape[-1])
    heads = lambda a: a.reshape(a.shape[0], a.shape[1], N_HEADS, HEAD_DIM)
    state = lambda h: h.reshape(h.shape[0], N_HEADS, HEAD_DIM, SSM_STATE)

    q_s, kf_s, vf_s, z_s, xbc_s, dt_s = _in_proj(flat(xs), w["w_in"], rows, rows, False)
    kf_s, vf_s, xbc_s = per_b(kf_s), per_b(vf_s), per_b(xbc_s)
    q, kp, vp, kf, vf, z, xbc, dt = _in_proj(xp, w["w_in"], ROW_TILE, keep, True)
    attn = _attn(q, kp, vp)

    cache_t = lambda c: jnp.transpose(c, (0, 2, 3, 1))
    rider = (per_b(q_s), kf_s, vf_s, cache_t(cache_k), cache_t(cache_v))
    h0 = jnp.zeros((bp, D_SSM, SSM_STATE), F32)
    pre = jnp.zeros((bp, CONV_PRE, D_CONV), F32)
    y_ssm, h_fin, attn_s = _ssd(xbc, dt, z, h0, pre, *_ssd_weights(w), rider=rider)
    yp = _mix_ffn(attn, y_ssm, xp, w, ROW_TILE, alpha, True)
    conv_new = xbc[:, t - (CONV_WIDTH - 1):, :]

    pre_s = jnp.pad(state_conv.astype(F32), ((0, 0), (CONV_PRE - (CONV_WIDTH - 1), 0), (0, 0)))
    y_ssm_s, h_fin_s = _ssd(xbc_s, per_b(dt_s), per_b(z_s), state_ssm.reshape(bs, D_SSM, SSM_STATE).astype(F32),
                            pre_s, *_ssd_weights(w))
    ys = _mix_ffn(flat(attn_s), flat(y_ssm_s), flat(xs), w, rows, alpha, False)
    conv_new_s = jnp.concatenate([state_conv.astype(F32), xbc_s], axis=1)[:, -(CONV_WIDTH - 1):, :]
    return (yp, per_b(ys[0]), heads(kf), heads(vf), state(h_fin), conv_new,
            heads(kf_s), heads(vf_s), state(h_fin_s), conv_new_s)


def kernel(x_prompt, x_sample, cache_k_win, cache_v_win, state_ssm, state_conv, w_in, conv_w, conv_b, dt_bias, a_log, d_skip, attn_norm_g, ssm_norm_g, w_out, ln1_g, ln1_b, w_gate, w_up, w_down, ln2_g, ln2_b):
    depth = w_in.shape[0]
    alpha = (2.0 * depth) ** 0.25
    hp, hs = x_prompt, x_sample
    cols = [[] for _ in range(8)]
    for l in range(depth):
        w = _layer_weights(w_in[l], conv_w[l], conv_b[l], dt_bias[l], a_log[l], d_skip[l], attn_norm_g[l],
                           ssm_norm_g[l], w_out[l], ln1_g[l], ln1_b[l], w_gate[l], w_up[l], w_down[l],
                           ln2_g[l], ln2_b[l])
        hp, hs, *layer_state = _layer(hp, hs, cache_k_win[l], cache_v_win[l], state_ssm[l], state_conv[l], w, alpha)
        for dst, val in zip(cols, layer_state):
            dst.append(val)
    return (hp, hs) + tuple(jnp.stack(c) for c in cols)
```

```python
import functools
import math

import numpy as np
import jax
import jax.numpy as jnp
from jax import lax
from jax.experimental import pallas as pl
from jax.experimental.pallas import tpu as pltpu

F32 = jnp.float32
BF16 = jnp.bfloat16

D_MODEL = 1024
HEAD_DIM = 64
N_HEADS = 16
N_PAIRS = N_HEADS // 2
D_ATT = N_HEADS * HEAD_DIM
D_SSM = N_HEADS * HEAD_DIM
SSM_STATE = 128
N_GROUPS = 4
HEADS_PER_GROUP = N_HEADS // N_GROUPS
GROUP_W = HEADS_PER_GROUP * HEAD_DIM
CONV_WIDTH = 4
D_CONV = D_SSM + 2 * N_GROUPS * SSM_STATE
D_FF = 2816
FF_CHUNK = D_FF // 2
CHUNK = 128
CONV_PRE = 8
CONV_PITCH = 2
SSD_GROUP = 2
DILATED_BRANCHES = ((128, 1), (512, 4), (2048, 16))
WIN = 128
MAX_WINDOW = 2048
LN_EPS = 1e-5
RMS_EPS = 1e-5
LANES = 128
DT_PAD = LANES
LOG2E = math.log2(math.e)
Q_SCALE = HEAD_DIM ** -0.5 * LOG2E
NEG = -1e30
VMEM_LIMIT = 56 * 1024 * 1024
ROW_TILE = 512


def _alibi_slopes():
    return 2.0 ** (-8.0 * np.arange(1, N_HEADS + 1) / N_HEADS)


def _params(sem, vmem=None):
    return pltpu.CompilerParams(dimension_semantics=sem, vmem_limit_bytes=vmem)


def _const_spec(shape):
    nd = len(shape)
    return pl.BlockSpec(shape, lambda *_: (0,) * nd)


def _split_bf16(x):
    hi = x.astype(BF16)
    lo = (x - hi.astype(F32)).astype(BF16)
    return hi, lo


def _dot(a, b):
    return jnp.dot(a, b, preferred_element_type=F32)


def _dot_nt(a, b):
    return lax.dot_general(a, b, (((1,), (1,)), ((), ())), preferred_element_type=F32)


def _dot_split(x, e):
    hi, lo = _split_bf16(x)
    return _dot(hi, e) + _dot(lo, e)


def _layer_norm(x, g, b):
    mu = jnp.mean(x, axis=-1, keepdims=True)
    xc = x - mu
    var = jnp.mean(xc * xc, axis=-1, keepdims=True)
    return xc * lax.rsqrt(var + LN_EPS) * g + b


def _rms_norm(x, g):
    return x * lax.rsqrt(jnp.mean(x * x, axis=-1, keepdims=True) + RMS_EPS) * g


def _silu(x):
    return x * (1.0 / (1.0 + jnp.exp(-x)))


def _in_proj_kernel(x_ref, w_ref, *out_refs, pair_major):
    xb = x_ref[...].astype(BF16)

    def seg(lo, hi):
        return _dot_nt(xb, w_ref[lo:hi, :])

    def put_pairs(ref, val):
        for hp in range(N_PAIRS):
            ref[hp] = val[:, hp * LANES:(hp + 1) * LANES]

    q = seg(0, D_ATT) * Q_SCALE
    k = seg(D_ATT, 2 * D_ATT)
    v = seg(2 * D_ATT, 3 * D_ATT)
    if pair_major:
        q_ref, kp_ref, vp_ref, kf_ref, vf_ref, z_ref, xbc_ref, dt_ref = out_refs
        put_pairs(q_ref, q)
        put_pairs(kp_ref, k)
        put_pairs(vp_ref, v)
    else:
        q_ref, kf_ref, vf_ref, z_ref, xbc_ref, dt_ref = out_refs
        q_ref[...] = q.astype(BF16)
    kf_ref[...] = k
    vf_ref[...] = v
    o = 3 * D_ATT
    z_ref[...] = seg(o, o + D_SSM).astype(BF16)
    o += D_SSM
    xbc_ref[...] = seg(o, o + D_CONV)
    o += D_CONV
    n_dt = w_ref.shape[0] - o
    w_dt = jnp.concatenate([w_ref[o:, :], jnp.zeros((DT_PAD - n_dt, D_MODEL), BF16)], axis=0)
    dt_ref[...] = _dot_nt(xb, w_dt)


def _in_proj(x, w_t, tm, keep, pair_major):
    b, t, _ = x.shape
    nt = t // tm
    skip = (t - keep) // tm
    row = lambda bi, i: (bi, i, 0)
    kept = lambda bi, i: (bi, jnp.maximum(i - skip, 0), 0)

    def spec(w, imap=row):
        return pl.BlockSpec((None, tm, w), imap)

    sds = jax.ShapeDtypeStruct
    pair_spec = pl.BlockSpec((None, N_PAIRS, tm, LANES), lambda bi, i: (bi, 0, i, 0))
    pair_shape = sds((b, N_PAIRS, t, LANES), F32)
    tail_specs = [spec(D_ATT, kept), spec(D_ATT, kept), spec(D_SSM), spec(D_CONV), spec(DT_PAD)]
    tail_shapes = [sds((b, keep, D_ATT), F32), sds((b, keep, D_ATT), F32),
                   sds((b, t, D_SSM), BF16), sds((b, t, D_CONV), F32), sds((b, t, DT_PAD), F32)]
    if pair_major:
        out_specs = [pair_spec] * 3 + tail_specs
        out_shape = [pair_shape] * 3 + tail_shapes
    else:
        out_specs = [spec(D_ATT)] + tail_specs
        out_shape = [sds((b, t, D_ATT), BF16)] + tail_shapes
    return pl.pallas_call(
        functools.partial(_in_proj_kernel, pair_major=pair_major),
        grid=(b, nt),
        in_specs=[spec(D_MODEL),
                  pl.BlockSpec(w_t.shape, lambda bi, i: (0, 0), pipeline_mode=pl.Buffered(1))],
        out_specs=out_specs,
        out_shape=out_shape,
        compiler_params=_params(("arbitrary", "arbitrary"), VMEM_LIMIT),
        name="in_proj",
    )(x, w_t)


def _branch_bias(dil):
    r = np.arange(WIN)[:, None]
    c = np.arange(2 * WIN)[None, :]
    ju = WIN + r - c
    valid = (ju >= 0) & (ju <= WIN)
    slopes = _alibi_slopes()[:, None, None]
    base = -slopes * (ju * dil)[None] * LOG2E
    full = np.where(valid[None], base, NEG)
    first = np.where((valid & (c >= WIN))[None], base, NEG)
    return np.stack([full, first]).astype(np.float32)


def _attn_block(qf, kblk, vblk, bias_ref, var, head0):
    q2 = jnp.concatenate([jnp.where(head0, qf, 0.0), jnp.where(head0, 0.0, qf)], axis=0).astype(BF16)
    s = _dot_nt(q2, kblk) + jnp.concatenate([bias_ref[var, 0], bias_ref[var, 1]], axis=0)
    m = jnp.max(s, axis=1, keepdims=True)
    p = jnp.exp2(s - m).astype(BF16)
    r0 = _dot(p[:WIN], vblk)
    r1 = _dot(p[WIN:], vblk)
    den = jnp.where(head0, r0[:, LANES:], r1[:, LANES:])
    num = jnp.where(head0, r0[:, :LANES], r1[:, :LANES])
    lse = jnp.where(head0, m[:WIN], m[WIN:]) + jnp.log2(den)
    return num / den, lse


def _attn_kernel(q_ref, k_ref, v_ref, b1_ref, b4_ref, b16_ref, o_ref,
                 q4_sc, k4_sc, v4_sc, kb1_sc, va1_sc, kb4_sc, va4_sc,
                 o1_sc, l1_sc, o16_sc, l16_sc, *, t):
    t4 = t // 4
    t16 = t // 16
    n1 = t // WIN
    n4 = t4 // WIN
    lane = lax.broadcasted_iota(jnp.int32, (1, LANES), 1)
    head0 = lane < HEAD_DIM
    zeros_k = jnp.zeros((WIN, LANES), BF16)
    zeros_v = jnp.zeros((WIN, 2 * LANES), BF16)

    def rows(start, n=WIN):
        return pl.ds(start, n)

    def with_ones(vf):
        return jnp.concatenate([vf.astype(BF16), jnp.ones(vf.shape, BF16)], axis=1)


    kb1_sc[0:WIN, :] = zeros_k
    va1_sc[0:WIN, :] = zeros_v
    for c in range(4):
        kb4_sc[c, 0:WIN, :] = zeros_k
        va4_sc[c, 0:WIN, :] = zeros_v

    for j in range(n1):
        src = rows(j * WIN)
        dst = rows(WIN + j * WIN)
        kb1_sc[dst, :] = k_ref[src, :].astype(BF16)
        va1_sc[dst, :] = with_ones(v_ref[src, :])

    for c in range(4):
        for j in range(n4):
            src = pl.ds(c + 4 * WIN * j, WIN, stride=4)
            dst = rows(c * t4 + j * WIN)
            kc = k_ref[src, :]
            vc = v_ref[src, :]
            k4_sc[dst, :] = kc
            v4_sc[dst, :] = vc
            q4_sc[dst, :] = q_ref[src, :]
            kb4_sc[c, rows(WIN + j * WIN), :] = kc.astype(BF16)
            va4_sc[c, rows(WIN + j * WIN), :] = with_ones(vc)

    for j in range(n1):
        out, lse = _attn_block(q_ref[rows(j * WIN), :], kb1_sc[rows(j * WIN, 2 * WIN), :],
                               va1_sc[rows(j * WIN, 2 * WIN), :], b1_ref, int(j == 0), head0)
        o1_sc[rows(j * WIN), :] = out
        l1_sc[rows(j * WIN), :] = lse

    for cls in range(16):
        r4, c = divmod(cls, 4)
        base = r4 * t4 + c
        src = pl.ds(base, t16, stride=4)
        kfull = jnp.concatenate([zeros_k, k4_sc[src, :].astype(BF16)], axis=0)
        vfull = jnp.concatenate([zeros_v, with_ones(v4_sc[src, :])], axis=0)
        for j in range(t16 // WIN):
            dst = pl.ds(base + 4 * WIN * j, WIN, stride=4)
            out, lse = _attn_block(q4_sc[dst, :], kfull[j * WIN:(j + 2) * WIN], vfull[j * WIN:(j + 2) * WIN],
                                   b16_ref, int(j == 0), head0)
            o16_sc[dst, :] = out
            l16_sc[dst, :] = lse

    for c in range(4):
        for j in range(n4):
            cm = rows(c * t4 + j * WIN)
            nat = pl.ds(c + 4 * WIN * j, WIN, stride=4)
            o4, l4 = _attn_block(q4_sc[cm, :], kb4_sc[c, rows(j * WIN, 2 * WIN), :],
                                 va4_sc[c, rows(j * WIN, 2 * WIN), :], b4_ref, int(j == 0), head0)
            l1 = l1_sc[nat, :]
            l16 = l16_sc[cm, :]
            m = jnp.maximum(jnp.maximum(l1, l4), l16)
            e1 = jnp.exp2(l1 - m)
            e4 = jnp.exp2(l4 - m)
            e16 = jnp.exp2(l16 - m)
            acc = e1 * o1_sc[nat, :] + e4 * o4 + e16 * o16_sc[cm, :]
            o_ref[nat, :] = acc / (e1 + e4 + e16)


def _attn(q, k, v):
    b, npairs, t, _ = q.shape
    assert t % (16 * WIN) == 0
    slab = pl.BlockSpec((None, None, t, LANES), lambda bi, hp: (bi, hp, 0, 0))
    biases = [jnp.asarray(_branch_bias(dil)) for _, dil in DILATED_BRANCHES]
    bias_spec = pl.BlockSpec((2, 2, WIN, 2 * WIN), lambda bi, hp: (0, hp, 0, 0))
    t4 = t // 4
    f32_slab = pltpu.VMEM((t, LANES), F32)
    scratch = [f32_slab, f32_slab, f32_slab,
               pltpu.VMEM((t + WIN, LANES), BF16), pltpu.VMEM((t + WIN, 2 * LANES), BF16),
               pltpu.VMEM((4, t4 + WIN, LANES), BF16), pltpu.VMEM((4, t4 + WIN, 2 * LANES), BF16)] + [f32_slab] * 4
    return pl.pallas_call(
        functools.partial(_attn_kernel, t=t),
        grid=(b, npairs),
        in_specs=[slab, slab, slab, bias_spec, bias_spec, bias_spec],
        out_specs=slab,
        out_shape=jax.ShapeDtypeStruct((b, npairs, t, LANES), F32),
        scratch_shapes=scratch,
        compiler_params=_params(("arbitrary", "arbitrary"), VMEM_LIMIT),
        name="dilated_attn",
    )(q, k, v, *biases)


def _sample_table(wb, tn, kpad):
    i = np.arange(tn)[:, None]
    c = np.arange(wb + kpad)[None, :]
    dist = wb + i - c
    mult = np.zeros(dist.shape)
    for window, dil in DILATED_BRANCHES:
        mult += (dist >= 0) & (dist <= window) & (dist % dil == 0) & (c < wb + tn)
    logm = np.where(mult > 0, np.log2(np.maximum(mult, 1)), NEG)
    slopes = _alibi_slopes()[:, None, None]
    tab = np.where(mult[None] > 0, -slopes * dist[None] * LOG2E + logm[None], NEG)
    return tab.reshape(N_HEADS * tn, wb + kpad).astype(np.float32)


def _sample_attn_heads(q_ref, kn_ref, vn_ref, kt_ref, vt_ref, tab_ref, o_ref, *, tn):
    nh, _, wb = kt_ref.shape
    kpad = tab_ref.shape[1] - wb
    width = nh * HEAD_DIM
    rows = nh * tn
    row_head = lax.shift_right_logical(lax.broadcasted_iota(jnp.int32, (rows, width), 0), int(math.log2(tn)))
    lane_head = lax.shift_right_logical(lax.broadcasted_iota(jnp.int32, (rows, width), 1), int(math.log2(HEAD_DIM)))
    diag = row_head == lane_head
    qt = jnp.concatenate([q_ref[...].astype(F32)] * nh, axis=0)
    qbd = jnp.where(diag, qt, 0.0).astype(BF16)
    pad = jnp.zeros((kpad - tn, width), F32)
    k_new = jnp.concatenate([kn_ref[...], pad], axis=0).astype(BF16)
    v_new = jnp.concatenate([vn_ref[...], pad], axis=0).astype(BF16)
    kt = kt_ref[...].reshape(width, wb).astype(BF16)
    vt = vt_ref[...].reshape(width, wb).astype(BF16)
    s = jnp.concatenate([_dot(qbd, kt), _dot_nt(qbd, k_new)], axis=1) + tab_ref[...]
    m = jnp.max(s, axis=1, keepdims=True)
    p = jnp.exp2(s - m).astype(BF16)
    den = jnp.sum(p.astype(F32), axis=1, keepdims=True)
    full = (_dot_nt(p[:, :wb], vt) + _dot(p[:, wb:], v_new)) / den
    full = jnp.where(diag, full, 0.0)
    acc = full[0:tn]
    for h in range(1, nh):
        acc = acc + full[h * tn:(h + 1) * tn]
    o_ref[...] = acc


def _expand_matrix(width):
    e = np.zeros((LANES, N_HEADS * width), np.float32)
    for h in range(N_HEADS):
        e[h, h * width:(h + 1) * width] = 1.0
    return e


def _ssd_kernel(*refs, l_in, n_chunks, group, rider_tn):
    (xbc_ref, dt_ref, z_ref, h0_ref, pre_ref, convw_ref, convb_ref, dtb_ref, a_ref, dskip_ref,
     gn_ref, tri_ref, e128_ref, e64_ref) = refs[:14]
    if rider_tn:
        rider_in = refs[14:20]
        y_ref, hout_ref, rider_out, xpad_sc, tail_sc, h_sc = refs[20:]
        _sample_attn_heads(*rider_in, rider_out, tn=rider_tn)
    else:
        y_ref, hout_ref, xpad_sc, tail_sc, h_sc = refs[14:]
    c = pl.program_id(1)

    @pl.when(c == 0)
    def _():
        h_sc[...] = h0_ref[...]
        for g in range(group):
            for s in range(D_CONV // LANES):
                tail_sc[g, s] = pre_ref[g, :, s * LANES:(s + 1) * LANES]

    for g in range(group):
        _ssd_chunk(xbc_ref.at[g], dt_ref.at[g], z_ref.at[g], convw_ref, convb_ref, dtb_ref, a_ref, dskip_ref,
                   gn_ref, tri_ref, e128_ref, e64_ref, y_ref.at[g], xpad_sc.at[g], tail_sc.at[g], h_sc.at[g],
                   l_in=l_in, n_chunks=n_chunks)

    @pl.when(c == n_chunks - 1)
    def _():
        hout_ref[...] = h_sc[...]


def _ssd_chunk(xbc_ref, dt_ref, z_ref, convw_ref, convb_ref, dtb_ref, a_ref, dskip_ref, gn_ref, tri_ref, e128_ref,
               e64_ref, y_ref, xpad_sc, tail_sc, h_sc, *, l_in, n_chunks):
    n_slabs = D_CONV // LANES
    pad = CHUNK - l_in

    def time_rows(tau, n):
        return pl.ds(CONV_PITCH * (tau + CONV_PRE), n, stride=CONV_PITCH)

    def pad_rows(x):
        return x if pad == 0 else jnp.concatenate([x, jnp.zeros((pad, x.shape[1]), x.dtype)], axis=0)

    slabs = []
    for s in range(n_slabs):
        lanes = slice(s * LANES, (s + 1) * LANES)
        xpad_sc[s, time_rows(-CONV_PRE, CONV_PRE), :] = tail_sc[s]
        xpad_sc[s, time_rows(0, l_in), :] = xbc_ref[:, lanes]
        acc = convb_ref[:, lanes]
        for w in range(CONV_WIDTH):
            acc = acc + convw_ref[w:w + 1, lanes] * xpad_sc[s, time_rows(w - (CONV_WIDTH - 1), l_in), :]
        slabs.append(_silu(acc))
        if n_chunks > 1:
            tail_sc[s] = xbc_ref[CHUNK - CONV_PRE:, lanes]
    n_x = D_SSM // LANES
    n_bc = N_GROUPS * SSM_STATE // LANES
    xs = jnp.concatenate(slabs[:n_x], axis=1)
    bmat = pad_rows(jnp.concatenate(slabs[n_x:n_x + n_bc], axis=1)).astype(BF16)
    cmat = jnp.concatenate(slabs[n_x + n_bc:], axis=1).astype(BF16)

    draw = dt_ref[...] + dtb_ref[...]
    dt = jnp.maximum(draw, 0.0) + jnp.log1p(jnp.exp(-jnp.abs(draw)))
    la = pad_rows(dt * (a_ref[...] * LOG2E))
    tri = tri_ref[...]
    la_hi, la_lo = _split_bf16(la)
    cs = _dot(tri, la_hi) + _dot(tri, la_lo)
    cs_t = cs.T
    cs_q = cs[:l_in]
    cs128 = _dot_split(cs_q, e128_ref[...])
    cs64 = _dot_split(cs_q, e64_ref[...])
    dt64 = _dot_split(dt, e64_ref[...])
    xdt = xs * dt64
    decay_in = jnp.exp2(cs64)
    cs_end = cs64[l_in - 1:l_in, :]
    xw = pad_rows(xdt * jnp.exp2(cs_end - cs64))
    xdt_k = pad_rows(xdt)

    lane = lax.broadcasted_iota(jnp.int32, (1, LANES), 1)
    head0 = lane < HEAD_DIM
    causal = (lax.broadcasted_iota(jnp.int32, (l_in, CHUNK), 0)
              >= lax.broadcasted_iota(jnp.int32, (l_in, CHUNK), 1))

    y_parts = []
    for g in range(N_GROUPS):
        gs = slice(g * SSM_STATE, (g + 1) * SSM_STATE)
        xg = slice(g * GROUP_W, (g + 1) * GROUP_W)
        b_g = bmat[:, gs]
        c_g = cmat[:, gs]
        cb = _dot_nt(c_g, b_g)
        h_prev = h_sc[xg, :]
        y_off = _dot_nt(c_g, h_prev.astype(BF16)) * decay_in[:, xg]
        y_diag = []
        for hp in range(HEADS_PER_GROUP // 2):
            gmats = []
            for j in range(2):
                h = g * HEADS_PER_GROUP + 2 * hp + j
                seg = cs128[:, h * LANES:(h + 1) * LANES] - cs_t[h:h + 1, :]
                gmats.append((cb * jnp.exp2(jnp.where(causal, seg, NEG))).astype(BF16))
            xp = xdt_k[:, (g * 2 + hp) * LANES:(g * 2 + hp + 1) * LANES]
            rhs = jnp.concatenate([jnp.where(head0, xp, 0.0), jnp.where(head0, 0.0, xp)], axis=0).astype(BF16)
            y_diag.append(_dot(jnp.concatenate(gmats, axis=1), rhs))
        y_parts.append(jnp.concatenate(y_diag, axis=1) + y_off)
        st = _dot(xw[:, xg].T.astype(BF16), b_g)
        for j in range(HEADS_PER_GROUP):
            h = g * HEADS_PER_GROUP + j
            hs = slice(h * HEAD_DIM, (h + 1) * HEAD_DIM)
            dec = jnp.exp2(cs128[l_in - 1:l_in, h * LANES:(h + 1) * LANES])
            h_sc[hs, :] = h_sc[hs, :] * dec + st[j * HEAD_DIM:(j + 1) * HEAD_DIM, :]

    y = jnp.concatenate(y_parts, axis=1) + dskip_ref[...] * xs
    y = y * _silu(z_ref[...].astype(F32))
    y_ref[...] = _rms_norm(y, gn_ref[...]).astype(BF16)


def _ssd(xbc, dt, z, h0, pre, conv_w, conv_b, dtb_pad, a_pad, dskip64, gnorm, rider=None):
    b, t, _ = xbc.shape
    l_in = min(CHUNK, t)
    nc = t // l_in
    tri = jnp.asarray(np.tril(np.ones((CHUNK, CHUNK), np.float32)), BF16)
    e128 = jnp.asarray(_expand_matrix(LANES), BF16)
    e64 = jnp.asarray(_expand_matrix(HEAD_DIM), BF16)
    group = math.gcd(b, SSD_GROUP)
    step = lambda w: pl.BlockSpec((group, l_in, w), lambda bi, c: (bi, c, 0))
    per_b = lambda r, w: pl.BlockSpec((group, r, w), lambda bi, c: (bi, 0, 0))
    consts = [conv_w, conv_b, dtb_pad, a_pad, dskip64, gnorm, tri, e128, e64]
    args = [xbc, dt, z, h0, pre] + consts
    in_specs = ([step(D_CONV), step(DT_PAD), step(D_SSM), per_b(D_SSM, SSM_STATE), per_b(CONV_PRE, D_CONV)]
                + [_const_spec(a.shape) for a in consts])
    out_specs = [step(D_SSM), per_b(D_SSM, SSM_STATE)]
    out_shape = [jax.ShapeDtypeStruct((b, t, D_SSM), BF16), jax.ShapeDtypeStruct((b, D_SSM, SSM_STATE), F32)]
    rider_tn = 0
    if rider is not None:
        q, k_new, v_new, cache_kt, cache_vt = rider
        bs, rider_tn, _ = q.shape
        wb = cache_kt.shape[-1]
        steps = (b // group) * nc
        parts = steps // bs
        nh = N_HEADS // parts
        assert steps == parts * bs and nh * parts == N_HEADS and nh % 2 == 0 and rider_tn & (rider_tn - 1) == 0
        tab = jnp.asarray(_sample_table(wb, rider_tn, LANES))
        seq = lambda bi, c: (bi * nc + c) // parts
        part = lambda bi, c: (bi * nc + c) % parts
        new = pl.BlockSpec((None, rider_tn, nh * HEAD_DIM), lambda bi, c: (seq(bi, c), 0, part(bi, c)))
        cache = pl.BlockSpec((None, nh, HEAD_DIM, wb), lambda bi, c: (seq(bi, c), part(bi, c), 0, 0))
        args += [q, k_new, v_new, cache_kt, cache_vt, tab]
        in_specs += [new, new, new, cache, cache,
                     pl.BlockSpec((nh * rider_tn, tab.shape[1]), lambda bi, c: (part(bi, c), 0))]
        out_specs.append(new)
        out_shape.append(jax.ShapeDtypeStruct((bs, rider_tn, D_ATT), F32))
    return pl.pallas_call(
        functools.partial(_ssd_kernel, l_in=l_in, n_chunks=nc, group=group, rider_tn=rider_tn),
        grid=(b // group, nc),
        in_specs=in_specs,
        out_specs=out_specs,
        out_shape=out_shape,
        scratch_shapes=[pltpu.VMEM((group, D_CONV // LANES, CONV_PITCH * (CONV_PRE + CHUNK), LANES), F32),
                        pltpu.VMEM((group, D_CONV // LANES, CONV_PRE, LANES), F32),
                        pltpu.VMEM((group, D_SSM, SSM_STATE), F32)],
        compiler_params=_params(("arbitrary", "arbitrary"), VMEM_LIMIT),
        name="ssd",
    )(*args)


def _mix_ffn_kernel(a_ref, ys_ref, x_ref, ga_ref, wo_ref, g1_ref, b1_ref, wg_ref, wu_ref, wd_ref, g2_ref, b2_ref,
                    out_ref, *, pair_major, alpha):
    if pair_major:
        attn = jnp.concatenate([a_ref[hp] for hp in range(N_PAIRS)], axis=1)
    else:
        attn = a_ref[...]
    a = _rms_norm(attn, ga_ref[...]).astype(BF16)
    mix = _dot(a, wo_ref[0:D_ATT, :]) + _dot(ys_ref[...], wo_ref[D_ATT:, :])
    h = _layer_norm(alpha * x_ref[...] + mix, g1_ref[...], b1_ref[...])
    hb = h.astype(BF16)
    acc = None
    for c in range(D_FF // FF_CHUNK):
        cols = slice(c * FF_CHUNK, (c + 1) * FF_CHUNK)
        act = (_silu(_dot(hb, wg_ref[:, cols])) * _dot(hb, wu_ref[:, cols])).astype(BF16)
        part = _dot(act, wd_ref[cols, :])
        acc = part if acc is None else acc + part
    out_ref[...] = _layer_norm(alpha * h + acc, g2_ref[...], b2_ref[...])


def _mix_ffn(attn, y_ssm, x, w, tm, alpha, pair_major):
    b, t, _ = x.shape
    row = lambda wd: pl.BlockSpec((None, tm, wd), lambda bi, i: (bi, i, 0))
    a_spec = pl.BlockSpec((None, N_PAIRS, tm, LANES), lambda bi, i: (bi, 0, i, 0)) if pair_major else row(D_ATT)
    single = lambda arr: pl.BlockSpec(arr.shape, lambda bi, i: (0, 0), pipeline_mode=pl.Buffered(1))
    vec = lambda arr: _const_spec(arr.shape)
    return pl.pallas_call(
        functools.partial(_mix_ffn_kernel, pair_major=pair_major, alpha=alpha),
        grid=(b, t // tm),
        in_specs=[a_spec, row(D_SSM), row(D_MODEL), vec(w["g_attn"]), single(w["w_out"]), vec(w["ln1_g"]),
                  vec(w["ln1_b"]), single(w["w_gate"]), single(w["w_up"]), single(w["w_down"]), vec(w["ln2_g"]),
                  vec(w["ln2_b"])],
        out_specs=row(D_MODEL),
        out_shape=jax.ShapeDtypeStruct((b, t, D_MODEL), F32),
        compiler_params=_params(("arbitrary", "arbitrary"), VMEM_LIMIT),
        name="mix_ffn",
    )(attn, y_ssm, x, w["g_attn"], w["w_out"], w["ln1_g"], w["ln1_b"], w["w_gate"], w["w_up"], w["w_down"],
      w["ln2_g"], w["ln2_b"])


def _pad_lanes(v, width=LANES):
    return jnp.pad(v.astype(F32), (0, width - v.shape[0])).reshape(1, width)


def _layer_weights(w_in, conv_w, conv_b, dt_bias, a_log, d_skip, attn_norm_g, ssm_norm_g, w_out, ln1_g, ln1_b,
                   w_gate, w_up, w_down, ln2_g, ln2_b):
    row = lambda v: v.astype(F32).reshape(1, -1)
    return dict(
        w_in=w_in.T.astype(BF16),
        conv_w=conv_w.astype(F32), conv_b=row(conv_b),
        dtb=_pad_lanes(dt_bias), a=_pad_lanes(-jnp.exp(a_log.astype(F32))),
        dskip=row(jnp.repeat(d_skip.astype(F32), HEAD_DIM)),
        g_attn=row(attn_norm_g), g_ssm=row(ssm_norm_g), w_out=w_out.astype(BF16),
        ln1_g=row(ln1_g), ln1_b=row(ln1_b), w_gate=w_gate.astype(BF16), w_up=w_up.astype(BF16),
        w_down=w_down.astype(BF16), ln2_g=row(ln2_g), ln2_b=row(ln2_b))


def _ssd_weights(w):
    return w["conv_w"], w["conv_b"], w["dtb"], w["a"], w["dskip"], w["g_ssm"]


def _layer(xp, xs, cache_k, cache_v, state_ssm, state_conv, w, alpha):
    bp, t, _ = xp.shape
    bs, tn, _ = xs.shape
    keep = min(MAX_WINDOW, t)
    rows = bs * tn
    flat = lambda a: a.reshape(1, rows, a.shape[-1])
    per_b = lambda a: a.reshape(bs, tn, a.shape[-1])
    heads = lambda a: a.reshape(a.shape[0], a.shape[1], N_HEADS, HEAD_DIM)
    state = lambda h: h.reshape(h.shape[0], N_HEADS, HEAD_DIM, SSM_STATE)

    q_s, kf_s, vf_s, z_s, xbc_s, dt_s = _in_proj(flat(xs), w["w_in"], rows, rows, False)
    kf_s, vf_s, xbc_s = per_b(kf_s), per_b(vf_s), per_b(xbc_s)
    q, kp, vp, kf, vf, z, xbc, dt = _in_proj(xp, w["w_in"], ROW_TILE, keep, True)
    attn = _attn(q, kp, vp)

    cache_t = lambda c: jnp.transpose(c, (0, 2, 3, 1))
    rider = (per_b(q_s), kf_s, vf_s, cache_t(cache_k), cache_t(cache_v))
    h0 = jnp.zeros((bp, D_SSM, SSM_STATE), F32)
    pre = jnp.zeros((bp, CONV_PRE, D_CONV), F32)
    y_ssm, h_fin, attn_s = _ssd(xbc, dt, z, h0, pre, *_ssd_weights(w), rider=rider)
    yp = _mix_ffn(attn, y_ssm, xp, w, ROW_TILE, alpha, True)
    conv_new = xbc[:, t - (CONV_WIDTH - 1):, :]

    pre_s = jnp.pad(state_conv.astype(F32), ((0, 0), (CONV_PRE - (CONV_WIDTH - 1), 0), (0, 0)))
    y_ssm_s, h_fin_s = _ssd(xbc_s, per_b(dt_s), per_b(z_s), state_ssm.reshape(bs, D_SSM, SSM_STATE).astype(F32),
                            pre_s, *_ssd_weights(w))
    ys = _mix_ffn(flat(attn_s), flat(y_ssm_s), flat(xs), w, rows, alpha, False)
    conv_new_s = jnp.concatenate([state_conv.astype(F32), xbc_s], axis=1)[:, -(CONV_WIDTH - 1):, :]
    return (yp, per_b(ys[0]), heads(kf), heads(vf), state(h_fin), conv_new,
            heads(kf_s), heads(vf_s), state(h_fin_s), conv_new_s)


def kernel(x_prompt, x_sample, cache_k_win, cache_v_win, state_ssm, state_conv, w_in, conv_w, conv_b, dt_bias, a_log, d_skip, attn_norm_g, ssm_norm_g, w_out, ln1_g, ln1_b, w_gate, w_up, w_down, ln2_g, ln2_b):
    depth = w_in.shape[0]
    alpha = (2.0 * depth) ** 0.25
    hp, hs = x_prompt, x_sample
    cols = [[] for _ in range(8)]
    for l in range(depth):
        w = _layer_weights(w_in[l], conv_w[l], conv_b[l], dt_bias[l], a_log[l], d_skip[l], attn_norm_g[l],
                           ssm_norm_g[l], w_out[l], ln1_g[l], ln1_b[l], w_gate[l], w_up[l], w_down[l],
                           ln2_g[l], ln2_b[l])
        hp, hs, *layer_state = _layer(hp, hs, cache_k_win[l], cache_v_win[l], state_ssm[l], state_conv[l], w, alpha)
        for dst, val in zip(cols, layer_state):
            dst.append(val)
    return (hp, hs) + tuple(jnp.stack(c) for c in cols)
```

```python
import functools
import math

import numpy as np
import jax
import jax.numpy as jnp
from jax import lax
from jax.experimental import pallas as pl
from jax.experimental.pallas import tpu as pltpu

F32 = jnp.float32
BF16 = jnp.bfloat16

D_MODEL = 1024
HEAD_DIM = 64
N_HEADS = 16
N_PAIRS = N_HEADS // 2
D_ATT = N_HEADS * HEAD_DIM
D_SSM = N_HEADS * HEAD_DIM
SSM_STATE = 128
N_GROUPS = 4
HEADS_PER_GROUP = N_HEADS // N_GROUPS
GROUP_W = HEADS_PER_GROUP * HEAD_DIM
CONV_WIDTH = 4
D_CONV = D_SSM + 2 * N_GROUPS * SSM_STATE
D_FF = 2816
CHUNK = 128
CONV_PRE = 8
CONV_PITCH = 2
SSD_GROUP = 2
DILATED_BRANCHES = ((128, 1), (512, 4), (2048, 16))
WIN = 128
MAX_WINDOW = 2048
LN_EPS = 1e-5
RMS_EPS = 1e-5
LANES = 128
DT_PAD = LANES
LOG2E = math.log2(math.e)
Q_SCALE = HEAD_DIM ** -0.5 * LOG2E
NEG = -1e30
VMEM_LIMIT = 56 * 1024 * 1024
ROW_TILE = 512


def _alibi_slopes():
    return 2.0 ** (-8.0 * np.arange(1, N_HEADS + 1) / N_HEADS)


def _params(sem, vmem=None):
    return pltpu.CompilerParams(dimension_semantics=sem, vmem_limit_bytes=vmem)


def _const_spec(shape):
    nd = len(shape)
    return pl.BlockSpec(shape, lambda *_: (0,) * nd)


def _split_bf16(x):
    hi = x.astype(BF16)
    lo = (x - hi.astype(F32)).astype(BF16)
    return hi, lo


def _dot(a, b):
    return jnp.dot(a, b, preferred_element_type=F32)


def _dot_nt(a, b):
    return lax.dot_general(a, b, (((1,), (1,)), ((), ())), preferred_element_type=F32)


def _dot_split(x, e):
    hi, lo = _split_bf16(x)
    return _dot(hi, e) + _dot(lo, e)


def _layer_norm(x, g, b):
    mu = jnp.mean(x, axis=-1, keepdims=True)
    xc = x - mu
    var = jnp.mean(xc * xc, axis=-1, keepdims=True)
    return xc * lax.rsqrt(var + LN_EPS) * g + b


def _rms_norm(x, g):
    return x * lax.rsqrt(jnp.mean(x * x, axis=-1, keepdims=True) + RMS_EPS) * g


def _silu(x):
    h = 0.5 * x
    return h + h * jnp.tanh(h)


def _in_proj_kernel(x_ref, w_ref, *out_refs, pair_major):
    xb = x_ref[...].astype(BF16)

    def seg(lo, hi):
        return _dot_nt(xb, w_ref[lo:hi, :])

    def put_pairs(ref, val):
        for hp in range(N_PAIRS):
            ref[hp] = val[:, hp * LANES:(hp + 1) * LANES]

    q = seg(0, D_ATT) * Q_SCALE
    k = seg(D_ATT, 2 * D_ATT)
    v = seg(2 * D_ATT, 3 * D_ATT)
    if pair_major:
        q_ref, kp_ref, vp_ref, kf_ref, vf_ref, z_ref, xbc_ref, dt_ref = out_refs
        put_pairs(q_ref, q)
        put_pairs(kp_ref, k)
        put_pairs(vp_ref, v)
    else:
        q_ref, kf_ref, vf_ref, z_ref, xbc_ref, dt_ref = out_refs
        q_ref[...] = q.astype(BF16)
    kf_ref[...] = k
    vf_ref[...] = v
    o = 3 * D_ATT
    z_ref[...] = seg(o, o + D_SSM).astype(BF16)
    o += D_SSM
    xbc_ref[...] = seg(o, o + D_CONV)
    o += D_CONV
    n_dt = w_ref.shape[0] - o
    w_dt = jnp.concatenate([w_ref[o:, :], jnp.zeros((DT_PAD - n_dt, D_MODEL), BF16)], axis=0)
    dt_ref[...] = _dot_nt(xb, w_dt)


def _in_proj(x, w_t, tm, keep, pair_major):
    b, t, _ = x.shape
    nt = t // tm
    skip = (t - keep) // tm
    row = lambda bi, i: (bi, i, 0)
    kept = lambda bi, i: (bi, jnp.maximum(i - skip, 0), 0)

    def spec(w, imap=row):
        return pl.BlockSpec((None, tm, w), imap)

    sds = jax.ShapeDtypeStruct
    pair_spec = pl.BlockSpec((None, N_PAIRS, tm, LANES), lambda bi, i: (bi, 0, i, 0))
    pair_shape = sds((b, N_PAIRS, t, LANES), F32)
    tail_specs = [spec(D_ATT, kept), spec(D_ATT, kept), spec(D_SSM), spec(D_CONV), spec(DT_PAD)]
    tail_shapes = [sds((b, keep, D_ATT), F32), sds((b, keep, D_ATT), F32),
                   sds((b, t, D_SSM), BF16), sds((b, t, D_CONV), F32), sds((b, t, DT_PAD), F32)]
    if pair_major:
        out_specs = [pair_spec] * 3 + tail_specs
        out_shape = [pair_shape] * 3 + tail_shapes
    else:
        out_specs = [spec(D_ATT)] + tail_specs
        out_shape = [sds((b, t, D_ATT), BF16)] + tail_shapes
    return pl.pallas_call(
        functools.partial(_in_proj_kernel, pair_major=pair_major),
        grid=(b, nt),
        in_specs=[spec(D_MODEL),
                  pl.BlockSpec(w_t.shape, lambda bi, i: (0, 0), pipeline_mode=pl.Buffered(1))],
        out_specs=out_specs,
        out_shape=out_shape,
        compiler_params=_params(("arbitrary", "arbitrary"), VMEM_LIMIT),
        name="in_proj",
    )(x, w_t)


def _branch_bias(dil):
    r = np.arange(WIN)[:, None]
    c = np.arange(2 * WIN)[None, :]
    ju = WIN + r - c
    valid = (ju >= 0) & (ju <= WIN)
    slopes = _alibi_slopes()[:, None, None]
    base = -slopes * (ju * dil)[None] * LOG2E
    full = np.where(valid[None], base, NEG)
    first = np.where((valid & (c >= WIN))[None], base, NEG)
    return np.stack([full, first]).astype(np.float32)


def _attn_block(qf, kblk, vblk, bias_ref, var, head0):
    q2 = jnp.concatenate([jnp.where(head0, qf, 0.0), jnp.where(head0, 0.0, qf)], axis=0).astype(BF16)
    s = _dot_nt(q2, kblk) + jnp.concatenate([bias_ref[var, 0], bias_ref[var, 1]], axis=0)
    m = jnp.max(s, axis=1, keepdims=True)
    p = jnp.exp2(s - m).astype(BF16)
    r0 = _dot(p[:WIN], vblk)
    r1 = _dot(p[WIN:], vblk)
    den = jnp.where(head0, r0[:, LANES:], r1[:, LANES:])
    num = jnp.where(head0, r0[:, :LANES], r1[:, :LANES])
    lse = jnp.where(head0, m[:WIN], m[WIN:]) + jnp.log2(den)
    return num / den, lse


def _attn_kernel(q_ref, k_ref, v_ref, b1_ref, b4_ref, b16_ref, o_ref,
                 q4_sc, k4_sc, v4_sc, kb1_sc, va1_sc, kb4_sc, va4_sc,
                 o1_sc, l1_sc, o16_sc, l16_sc, *, t):
    t4 = t // 4
    t16 = t // 16
    n1 = t // WIN
    n4 = t4 // WIN
    lane = lax.broadcasted_iota(jnp.int32, (1, LANES), 1)
    head0 = lane < HEAD_DIM
    zeros_k = jnp.zeros((WIN, LANES), BF16)
    zeros_v = jnp.zeros((WIN, 2 * LANES), BF16)

    def rows(start, n=WIN):
        return pl.ds(start, n)

    def with_ones(vf):
        return jnp.concatenate([vf.astype(BF16), jnp.ones(vf.shape, BF16)], axis=1)


    kb1_sc[0:WIN, :] = zeros_k
    va1_sc[0:WIN, :] = zeros_v
    for c in range(4):
        kb4_sc[c, 0:WIN, :] = zeros_k
        va4_sc[c, 0:WIN, :] = zeros_v

    for j in range(n1):
        src = rows(j * WIN)
        dst = rows(WIN + j * WIN)
        kb1_sc[dst, :] = k_ref[src, :].astype(BF16)
        va1_sc[dst, :] = with_ones(v_ref[src, :])

    for c in range(4):
        for j in range(n4):
            src = pl.ds(c + 4 * WIN * j, WIN, stride=4)
            dst = rows(c * t4 + j * WIN)
            kc = k_ref[src, :]
            vc = v_ref[src, :]
            k4_sc[dst, :] = kc
            v4_sc[dst, :] = vc
            q4_sc[dst, :] = q_ref[src, :]
            kb4_sc[c, rows(WIN + j * WIN), :] = kc.astype(BF16)
            va4_sc[c, rows(WIN + j * WIN), :] = with_ones(vc)

    for j in range(n1):
        out, lse = _attn_block(q_ref[rows(j * WIN), :], kb1_sc[rows(j * WIN, 2 * WIN), :],
                               va1_sc[rows(j * WIN, 2 * WIN), :], b1_ref, int(j == 0), head0)
        o1_sc[rows(j * WIN), :] = out
        l1_sc[rows(j * WIN), :] = lse

    for cls in range(16):
        r4, c = divmod(cls, 4)
        base = r4 * t4 + c
        src = pl.ds(base, t16, stride=4)
        kfull = jnp.concatenate([zeros_k, k4_sc[src, :].astype(BF16)], axis=0)
        vfull = jnp.concatenate([zeros_v, with_ones(v4_sc[src, :])], axis=0)
        for j in range(t16 // WIN):
            dst = pl.ds(base + 4 * WIN * j, WIN, stride=4)
            out, lse = _attn_block(q4_sc[dst, :], kfull[j * WIN:(j + 2) * WIN], vfull[j * WIN:(j + 2) * WIN],
                                   b16_ref, int(j == 0), head0)
            o16_sc[dst, :] = out
            l16_sc[dst, :] = lse

    for c in range(4):
        for j in range(n4):
            cm = rows(c * t4 + j * WIN)
            nat = pl.ds(c + 4 * WIN * j, WIN, stride=4)
            o4, l4 = _attn_block(q4_sc[cm, :], kb4_sc[c, rows(j * WIN, 2 * WIN), :],
                                 va4_sc[c, rows(j * WIN, 2 * WIN), :], b4_ref, int(j == 0), head0)
            l1 = l1_sc[nat, :]
            l16 = l16_sc[cm, :]
            m = jnp.maximum(jnp.maximum(l1, l4), l16)
            e1 = jnp.exp2(l1 - m)
            e4 = jnp.exp2(l4 - m)
            e16 = jnp.exp2(l16 - m)
            acc = e1 * o1_sc[nat, :] + e4 * o4 + e16 * o16_sc[cm, :]
            o_ref[nat, :] = acc / (e1 + e4 + e16)


def _attn(q, k, v):
    b, npairs, t, _ = q.shape
    assert t % (16 * WIN) == 0
    slab = pl.BlockSpec((None, None, t, LANES), lambda bi, hp: (bi, hp, 0, 0))
    biases = [jnp.asarray(_branch_bias(dil)) for _, dil in DILATED_BRANCHES]
    bias_spec = pl.BlockSpec((2, 2, WIN, 2 * WIN), lambda bi, hp: (0, hp, 0, 0))
    t4 = t // 4
    f32_slab = pltpu.VMEM((t, LANES), F32)
    scratch = [f32_slab, f32_slab, f32_slab,
               pltpu.VMEM((t + WIN, LANES), BF16), pltpu.VMEM((t + WIN, 2 * LANES), BF16),
               pltpu.VMEM((4, t4 + WIN, LANES), BF16), pltpu.VMEM((4, t4 + WIN, 2 * LANES), BF16)] + [f32_slab] * 4
    return pl.pallas_call(
        functools.partial(_attn_kernel, t=t),
        grid=(b, npairs),
        in_specs=[slab, slab, slab, bias_spec, bias_spec, bias_spec],
        out_specs=slab,
        out_shape=jax.ShapeDtypeStruct((b, npairs, t, LANES), F32),
        scratch_shapes=scratch,
        compiler_params=_params(("arbitrary", "arbitrary"), VMEM_LIMIT),
        name="dilated_attn",
    )(q, k, v, *biases)


def _sample_table(wb, tn, kpad):
    i = np.arange(tn)[:, None]
    c = np.arange(wb + kpad)[None, :]
    dist = wb + i - c
    mult = np.zeros(dist.shape)
    for window, dil in DILATED_BRANCHES:
        mult += (dist >= 0) & (dist <= window) & (dist % dil == 0) & (c < wb + tn)
    logm = np.where(mult > 0, np.log2(np.maximum(mult, 1)), NEG)
    slopes = _alibi_slopes()[:, None, None]
    tab = np.where(mult[None] > 0, -slopes * dist[None] * LOG2E + logm[None], NEG)
    return tab.reshape(N_HEADS * tn, wb + kpad).astype(np.float32)


def _sample_attn_heads(q_ref, kn_ref, vn_ref, kt_ref, vt_ref, tab_ref, o_ref, *, tn):
    nh, _, wb = kt_ref.shape
    kpad = tab_ref.shape[1] - wb
    width = nh * HEAD_DIM
    rows = nh * tn
    row_head = lax.shift_right_logical(lax.broadcasted_iota(jnp.int32, (rows, width), 0), int(math.log2(tn)))
    lane_head = lax.shift_right_logical(lax.broadcasted_iota(jnp.int32, (rows, width), 1), int(math.log2(HEAD_DIM)))
    diag = row_head == lane_head
    qt = jnp.concatenate([q_ref[...].astype(F32)] * nh, axis=0)
    qbd = jnp.where(diag, qt, 0.0).astype(BF16)
    pad = jnp.zeros((kpad - tn, width), F32)
    k_new = jnp.concatenate([kn_ref[...], pad], axis=0).astype(BF16)
    v_new = jnp.concatenate([vn_ref[...], pad], axis=0).astype(BF16)
    kt = kt_ref[...].reshape(width, wb).astype(BF16)
    vt = vt_ref[...].reshape(width, wb).astype(BF16)
    s = jnp.concatenate([_dot(qbd, kt), _dot_nt(qbd, k_new)], axis=1) + tab_ref[...]
    m = jnp.max(s, axis=1, keepdims=True)
    p = jnp.exp2(s - m).astype(BF16)
    den = jnp.sum(p.astype(F32), axis=1, keepdims=True)
    full = (_dot_nt(p[:, :wb], vt) + _dot(p[:, wb:], v_new)) / den
    full = jnp.where(diag, full, 0.0)
    acc = full[0:tn]
    for h in range(1, nh):
        acc = acc + full[h * tn:(h + 1) * tn]
    o_ref[...] = acc


def _expand_matrix(width):
    e = np.zeros((LANES, N_HEADS * width), np.float32)
    for h in range(N_HEADS):
        e[h, h * width:(h + 1) * width] = 1.0
    return e


def _ssd_kernel(*refs, l_in, n_chunks, group, rider_tn):
    (xbc_ref, dt_ref, z_ref, h0_ref, pre_ref, convw_ref, convb_ref, dtb_ref, a_ref, dskip_ref,
     gn_ref, tri_ref, e128_ref, e64_ref) = refs[:14]
    if rider_tn:
        rider_in = refs[14:20]
        y_ref, hout_ref, rider_out, xpad_sc, tail_sc, h_sc = refs[20:]
        _sample_attn_heads(*rider_in, rider_out, tn=rider_tn)
    else:
        y_ref, hout_ref, xpad_sc, tail_sc, h_sc = refs[14:]
    c = pl.program_id(1)

    @pl.when(c == 0)
    def _():
        h_sc[...] = h0_ref[...]
        for g in range(group):
            for s in range(D_CONV // LANES):
                tail_sc[g, s] = pre_ref[g, :, s * LANES:(s + 1) * LANES]

    for g in range(group):
        _ssd_chunk(xbc_ref.at[g], dt_ref.at[g], z_ref.at[g], convw_ref, convb_ref, dtb_ref, a_ref, dskip_ref,
                   gn_ref, tri_ref, e128_ref, e64_ref, y_ref.at[g], xpad_sc.at[g], tail_sc.at[g], h_sc.at[g],
                   l_in=l_in, n_chunks=n_chunks)

    @pl.when(c == n_chunks - 1)
    def _():
        hout_ref[...] = h_sc[...]


def _ssd_chunk(xbc_ref, dt_ref, z_ref, convw_ref, convb_ref, dtb_ref, a_ref, dskip_ref, gn_ref, tri_ref, e128_ref,
               e64_ref, y_ref, xpad_sc, tail_sc, h_sc, *, l_in, n_chunks):
    pad = CHUNK - l_in

    def time_rows(tau, n):
        return pl.ds(CONV_PITCH * (tau + CONV_PRE), n, stride=CONV_PITCH)

    def pad_rows(x):
        return x if pad == 0 else jnp.concatenate([x, jnp.zeros((pad, x.shape[1]), x.dtype)], axis=0)

    def conv_slab(s):
        lanes = slice(s * LANES, (s + 1) * LANES)
        xpad_sc[s, time_rows(-CONV_PRE, CONV_PRE), :] = tail_sc[s]
        xpad_sc[s, time_rows(0, l_in), :] = xbc_ref[:, lanes]
        acc = convb_ref[:, lanes]
        for w in range(CONV_WIDTH):
            acc = acc + convw_ref[w:w + 1, lanes] * xpad_sc[s, time_rows(w - (CONV_WIDTH - 1), l_in), :]
        if n_chunks > 1:
            tail_sc[s] = xbc_ref[CHUNK - CONV_PRE:, lanes]
        return _silu(acc)

    n_x = D_SSM // LANES
    n_bc = N_GROUPS * SSM_STATE // LANES

    draw = dt_ref[...] + dtb_ref[...]
    dt = jnp.maximum(draw, 0.0) + jnp.log1p(jnp.exp(-jnp.abs(draw)))
    la = pad_rows(dt * (a_ref[...] * LOG2E))
    tri = tri_ref[...]
    la_hi, la_lo = _split_bf16(la)
    cs = _dot(tri, la_hi) + _dot(tri, la_lo)
    cs_t = cs.T
    cs_parts = _split_bf16(cs[:l_in])
    dt_parts = _split_bf16(dt)

    def expand(parts, e_ref, lo, hi):
        e = e_ref[:, lo:hi]
        return _dot(parts[0], e) + _dot(parts[1], e)

    lane = lax.broadcasted_iota(jnp.int32, (1, LANES), 1)
    head0 = lane < HEAD_DIM
    causal = (lax.broadcasted_iota(jnp.int32, (l_in, CHUNK), 0)
              >= lax.broadcasted_iota(jnp.int32, (l_in, CHUNK), 1))

    y_parts = []
    sumsq = None
    for g in range(N_GROUPS):
        xg = slice(g * GROUP_W, (g + 1) * GROUP_W)
        xs = jnp.concatenate([conv_slab(2 * g), conv_slab(2 * g + 1)], axis=1)
        b_g = pad_rows(conv_slab(n_x + g)).astype(BF16)
        c_g = conv_slab(n_x + n_bc + g).astype(BF16)
        cs128 = expand(cs_parts, e128_ref, g * HEADS_PER_GROUP * LANES, (g + 1) * HEADS_PER_GROUP * LANES)
        cs64 = expand(cs_parts, e64_ref, g * GROUP_W, (g + 1) * GROUP_W)
        xdt = xs * expand(dt_parts, e64_ref, g * GROUP_W, (g + 1) * GROUP_W)
        xw = pad_rows(xdt * jnp.exp2(cs64[l_in - 1:l_in, :] - cs64))
        xdt_k = pad_rows(xdt)
        cb = _dot_nt(c_g, b_g)
        h_prev = h_sc[xg, :]
        y_g = _dot_nt(c_g, h_prev.astype(BF16)) * jnp.exp2(cs64)
        y_diag = []
        for hp in range(HEADS_PER_GROUP // 2):
            gmats = []
            for j in range(2):
                hl = 2 * hp + j
                seg = cs128[:, hl * LANES:(hl + 1) * LANES] - cs_t[g * HEADS_PER_GROUP + hl:g * HEADS_PER_GROUP + hl + 1, :]
                gmats.append((cb * jnp.exp2(jnp.where(causal, seg, NEG))).astype(BF16))
            xp = xdt_k[:, hp * LANES:(hp + 1) * LANES]
            rhs = jnp.concatenate([jnp.where(head0, xp, 0.0), jnp.where(head0, 0.0, xp)], axis=0).astype(BF16)
            y_diag.append(_dot(jnp.concatenate(gmats, axis=1), rhs))
        st = _dot(xw.T.astype(BF16), b_g)
        for j in range(HEADS_PER_GROUP):
            hs = slice(g * GROUP_W + j * HEAD_DIM, g * GROUP_W + (j + 1) * HEAD_DIM)
            dec = jnp.exp2(cs128[l_in - 1:l_in, j * LANES:(j + 1) * LANES])
            h_sc[hs, :] = h_sc[hs, :] * dec + st[j * HEAD_DIM:(j + 1) * HEAD_DIM, :]
        y_g = y_g + jnp.concatenate(y_diag, axis=1) + dskip_ref[:, xg] * xs
        y_g = y_g * _silu(z_ref[:, xg].astype(F32))
        part = jnp.sum(y_g * y_g, axis=-1, keepdims=True)
        sumsq = part if sumsq is None else sumsq + part
        y_parts.append(y_g)

    scale = lax.rsqrt(sumsq * (1.0 / D_SSM) + RMS_EPS)
    for g in range(N_GROUPS):
        xg = slice(g * GROUP_W, (g + 1) * GROUP_W)
        y_ref[:, xg] = (y_parts[g] * scale * gn_ref[:, xg]).astype(BF16)


def _ssd(xbc, dt, z, h0, pre, conv_w, conv_b, dtb_pad, a_pad, dskip64, gnorm, rider=None):
    b, t, _ = xbc.shape
    l_in = min(CHUNK, t)
    nc = t // l_in
    tri = jnp.asarray(np.tril(np.ones((CHUNK, CHUNK), np.float32)), BF16)
    e128 = jnp.asarray(_expand_matrix(LANES), BF16)
    e64 = jnp.asarray(_expand_matrix(HEAD_DIM), BF16)
    group = math.gcd(b, SSD_GROUP)
    step = lambda w: pl.BlockSpec((group, l_in, w), lambda bi, c: (bi, c, 0))
    per_b = lambda r, w: pl.BlockSpec((group, r, w), lambda bi, c: (bi, 0, 0))
    consts = [conv_w, conv_b, dtb_pad, a_pad, dskip64, gnorm, tri, e128, e64]
    args = [xbc, dt, z, h0, pre] + consts
    in_specs = ([step(D_CONV), step(DT_PAD), step(D_SSM), per_b(D_SSM, SSM_STATE), per_b(CONV_PRE, D_CONV)]
                + [_const_spec(a.shape) for a in consts])
    out_specs = [step(D_SSM), per_b(D_SSM, SSM_STATE)]
    out_shape = [jax.ShapeDtypeStruct((b, t, D_SSM), BF16), jax.ShapeDtypeStruct((b, D_SSM, SSM_STATE), F32)]
    rider_tn = 0
    if rider is not None:
        q, k_new, v_new, cache_kt, cache_vt = rider
        bs, rider_tn, _ = q.shape
        wb = cache_kt.shape[-1]
        steps = (b // group) * nc
        parts = steps // bs
        nh = N_HEADS // parts
        assert steps == parts * bs and nh * parts == N_HEADS and nh % 2 == 0 and rider_tn & (rider_tn - 1) == 0
        tab = jnp.asarray(_sample_table(wb, rider_tn, LANES))
        seq = lambda bi, c: (bi * nc + c) // parts
        part = lambda bi, c: (bi * nc + c) % parts
        new = pl.BlockSpec((None, rider_tn, nh * HEAD_DIM), lambda bi, c: (seq(bi, c), 0, part(bi, c)))
        cache = pl.BlockSpec((None, nh, HEAD_DIM, wb), lambda bi, c: (seq(bi, c), part(bi, c), 0, 0))
        args += [q, k_new, v_new, cache_kt, cache_vt, tab]
        in_specs += [new, new, new, cache, cache,
                     pl.BlockSpec((nh * rider_tn, tab.shape[1]), lambda bi, c: (part(bi, c), 0))]
        out_specs.append(new)
        out_shape.append(jax.ShapeDtypeStruct((bs, rider_tn, D_ATT), F32))
    return pl.pallas_call(
        functools.partial(_ssd_kernel, l_in=l_in, n_chunks=nc, group=group, rider_tn=rider_tn),
        grid=(b // group, nc),
        in_specs=in_specs,
        out_specs=out_specs,
        out_shape=out_shape,
        scratch_shapes=[pltpu.VMEM((group, D_CONV // LANES, CONV_PITCH * (CONV_PRE + CHUNK), LANES), F32),
                        pltpu.VMEM((group, D_CONV // LANES, CONV_PRE, LANES), F32),
                        pltpu.VMEM((group, D_SSM, SSM_STATE), F32)],
        compiler_params=_params(("arbitrary", "arbitrary"), VMEM_LIMIT),
        name="ssd",
    )(*args)


def _mix_ffn_kernel(a_ref, ys_ref, x_ref, ga_ref, wo_ref, g1_ref, b1_ref, wg_ref, wu_ref, wd_ref, g2_ref, b2_ref,
                    out_ref, *, pair_major, alpha):
    if pair_major:
        attn = jnp.concatenate([a_ref[hp] for hp in range(N_PAIRS)], axis=1)
    else:
        attn = a_ref[...]
    a = _rms_norm(attn, ga_ref[...]).astype(BF16)
    mix = _dot(a, wo_ref[0:D_ATT, :]) + _dot(ys_ref[...], wo_ref[D_ATT:, :])
    h = _layer_norm(alpha * x_ref[...] + mix, g1_ref[...], b1_ref[...])
    hb = h.astype(BF16)
    act = (_silu(_dot(hb, wg_ref[...])) * _dot(hb, wu_ref[...])).astype(BF16)
    out_ref[...] = _layer_norm(alpha * h + _dot(act, wd_ref[...]), g2_ref[...], b2_ref[...])


def _mix_ffn(attn, y_ssm, x, w, tm, alpha, pair_major):
    b, t, _ = x.shape
    row = lambda wd: pl.BlockSpec((None, tm, wd), lambda bi, i: (bi, i, 0))
    a_spec = pl.BlockSpec((None, N_PAIRS, tm, LANES), lambda bi, i: (bi, 0, i, 0)) if pair_major else row(D_ATT)
    single = lambda arr: pl.BlockSpec(arr.shape, lambda bi, i: (0, 0), pipeline_mode=pl.Buffered(1))
    vec = lambda arr: _const_spec(arr.shape)
    return pl.pallas_call(
        functools.partial(_mix_ffn_kernel, pair_major=pair_major, alpha=alpha),
        grid=(b, t // tm),
        in_specs=[a_spec, row(D_SSM), row(D_MODEL), vec(w["g_attn"]), single(w["w_out"]), vec(w["ln1_g"]),
                  vec(w["ln1_b"]), single(w["w_gate"]), single(w["w_up"]), single(w["w_down"]), vec(w["ln2_g"]),
                  vec(w["ln2_b"])],
        out_specs=row(D_MODEL),
        out_shape=jax.ShapeDtypeStruct((b, t, D_MODEL), F32),
        compiler_params=_params(("arbitrary", "arbitrary"), VMEM_LIMIT),
        name="mix_ffn",
    )(attn, y_ssm, x, w["g_attn"], w["w_out"], w["ln1_g"], w["ln1_b"], w["w_gate"], w["w_up"], w["w_down"],
      w["ln2_g"], w["ln2_b"])


def _pad_lanes(v, width=LANES):
    return jnp.pad(v.astype(F32), (0, width - v.shape[0])).reshape(1, width)


def _layer_weights(w_in, conv_w, conv_b, dt_bias, a_log, d_skip, attn_norm_g, ssm_norm_g, w_out, ln1_g, ln1_b,
                   w_gate, w_up, w_down, ln2_g, ln2_b):
    row = lambda v: v.astype(F32).reshape(1, -1)
    return dict(
        w_in=w_in.T.astype(BF16),
        conv_w=conv_w.astype(F32), conv_b=row(conv_b),
        dtb=_pad_lanes(dt_bias), a=_pad_lanes(-jnp.exp(a_log.astype(F32))),
        dskip=row(jnp.repeat(d_skip.astype(F32), HEAD_DIM)),
        g_attn=row(attn_norm_g), g_ssm=row(ssm_norm_g), w_out=w_out.astype(BF16),
        ln1_g=row(ln1_g), ln1_b=row(ln1_b), w_gate=w_gate.astype(BF16), w_up=w_up.astype(BF16),
        w_down=w_down.astype(BF16), ln2_g=row(ln2_g), ln2_b=row(ln2_b))


def _ssd_weights(w):
    return w["conv_w"], w["conv_b"], w["dtb"], w["a"], w["dskip"], w["g_ssm"]


def _layer(xp, xs, cache_k, cache_v, state_ssm, state_conv, w, alpha):
    bp, t, _ = xp.shape
    bs, tn, _ = xs.shape
    keep = min(MAX_WINDOW, t)
    rows = bs * tn
    flat = lambda a: a.reshape(1, rows, a.shape[-1])
    per_b = lambda a: a.reshape(bs, tn, a.shape[-1])
    heads = lambda a: a.reshape(a.shape[0], a.shape[1], N_HEADS, HEAD_DIM)
    state = lambda h: h.reshape(h.shape[0], N_HEADS, HEAD_DIM, SSM_STATE)

    q_s, kf_s, vf_s, z_s, xbc_s, dt_s = _in_proj(flat(xs), w["w_in"], rows, rows, False)
    kf_s, vf_s, xbc_s = per_b(kf_s), per_b(vf_s), per_b(xbc_s)
    q, kp, vp, kf, vf, z, xbc, dt = _in_proj(xp, w["w_in"], ROW_TILE, keep, True)
    attn = _attn(q, kp, vp)

    cache_t = lambda c: jnp.transpose(c, (0, 2, 3, 1))
    rider = (per_b(q_s), kf_s, vf_s, cache_t(cache_k), cache_t(cache_v))
    h0 = jnp.zeros((bp, D_SSM, SSM_STATE), F32)
    pre = jnp.zeros((bp, CONV_PRE, D_CONV), F32)
    y_ssm, h_fin, attn_s = _ssd(xbc, dt, z, h0, pre, *_ssd_weights(w), rider=rider)
    yp = _mix_ffn(attn, y_ssm, xp, w, ROW_TILE, alpha, True)
    conv_new = xbc[:, t - (CONV_WIDTH - 1):, :]

    pre_s = jnp.pad(state_conv.astype(F32), ((0, 0), (CONV_PRE - (CONV_WIDTH - 1), 0), (0, 0)))
    y_ssm_s, h_fin_s = _ssd(xbc_s, per_b(dt_s), per_b(z_s), state_ssm.reshape(bs, D_SSM, SSM_STATE).astype(F32),
                            pre_s, *_ssd_weights(w))
    ys = _mix_ffn(flat(attn_s), flat(y_ssm_s), flat(xs), w, rows, alpha, False)
    conv_new_s = jnp.concatenate([state_conv.astype(F32), xbc_s], axis=1)[:, -(CONV_WIDTH - 1):, :]
    return (yp, per_b(ys[0]), heads(kf), heads(vf), state(h_fin), conv_new,
            heads(kf_s), heads(vf_s), state(h_fin_s), conv_new_s)


def kernel(x_prompt, x_sample, cache_k_win, cache_v_win, state_ssm, state_conv, w_in, conv_w, conv_b, dt_bias, a_log, d_skip, attn_norm_g, ssm_norm_g, w_out, ln1_g, ln1_b, w_gate, w_up, w_down, ln2_g, ln2_b):
    depth = w_in.shape[0]
    alpha = (2.0 * depth) ** 0.25
    hp, hs = x_prompt, x_sample
    cols = [[] for _ in range(8)]
    for l in range(depth):
        w = _layer_weights(w_in[l], conv_w[l], conv_b[l], dt_bias[l], a_log[l], d_skip[l], attn_norm_g[l],
                           ssm_norm_g[l], w_out[l], ln1_g[l], ln1_b[l], w_gate[l], w_up[l], w_down[l],
                           ln2_g[l], ln2_b[l])
        hp, hs, *layer_state = _layer(hp, hs, cache_k_win[l], cache_v_win[l], state_ssm[l], state_conv[l], w, alpha)
        for dst, val in zip(cols, layer_state):
            dst.append(val)
    return (hp, hs) + tuple(jnp.stack(c) for c in cols)
```

```python
import functools
import math

import numpy as np
import jax
import jax.numpy as jnp
from jax import lax
from jax.experimental import pallas as pl
from jax.experimental.pallas import tpu as pltpu

F32 = jnp.float32
BF16 = jnp.bfloat16

D_MODEL = 1024
HEAD_DIM = 64
N_HEADS = 16
N_PAIRS = N_HEADS // 2
D_ATT = N_HEADS * HEAD_DIM
D_SSM = N_HEADS * HEAD_DIM
SSM_STATE = 128
N_GROUPS = 4
HEADS_PER_GROUP = N_HEADS // N_GROUPS
GROUP_W = HEADS_PER_GROUP * HEAD_DIM
CONV_WIDTH = 4
D_CONV = D_SSM + 2 * N_GROUPS * SSM_STATE
D_FF = 2816
CHUNK = 128
CONV_PRE = 8
CONV_PITCH = 2
SSD_GROUP = 2
DILATED_BRANCHES = ((128, 1), (512, 4), (2048, 16))
WIN = 128
MAX_WINDOW = 2048
LN_EPS = 1e-5
RMS_EPS = 1e-5
LANES = 128
DT_PAD = LANES
LOG2E = math.log2(math.e)
Q_SCALE = HEAD_DIM ** -0.5 * LOG2E
NEG = -1e30
VMEM_LIMIT = 56 * 1024 * 1024
ROW_TILE = 512


def _alibi_slopes():
    return 2.0 ** (-8.0 * np.arange(1, N_HEADS + 1) / N_HEADS)


def _params(sem, vmem=None):
    return pltpu.CompilerParams(dimension_semantics=sem, vmem_limit_bytes=vmem)


def _const_spec(shape):
    nd = len(shape)
    return pl.BlockSpec(shape, lambda *_: (0,) * nd)


def _split_bf16(x):
    hi = x.astype(BF16)
    lo = (x - hi.astype(F32)).astype(BF16)
    return hi, lo


def _dot(a, b):
    return jnp.dot(a, b, preferred_element_type=F32)


def _dot_nt(a, b):
    return lax.dot_general(a, b, (((1,), (1,)), ((), ())), preferred_element_type=F32)


def _dot_split(x, e):
    hi, lo = _split_bf16(x)
    return _dot(hi, e) + _dot(lo, e)


def _layer_norm(x, g, b):
    mu = jnp.mean(x, axis=-1, keepdims=True)
    xc = x - mu
    var = jnp.mean(xc * xc, axis=-1, keepdims=True)
    return xc * lax.rsqrt(var + LN_EPS) * g + b


def _rms_norm(x, g):
    return x * lax.rsqrt(jnp.mean(x * x, axis=-1, keepdims=True) + RMS_EPS) * g


def _silu(x):
    h = 0.5 * x
    return h + h * jnp.tanh(h)


def _in_proj_kernel(x_ref, w_ref, *out_refs, pair_major):
    xb = x_ref[...].astype(BF16)

    def seg(lo, hi):
        return _dot_nt(xb, w_ref[lo:hi, :])

    def put_pairs(ref, val):
        for hp in range(N_PAIRS):
            ref[hp] = val[:, hp * LANES:(hp + 1) * LANES]

    q = seg(0, D_ATT) * Q_SCALE
    k = seg(D_ATT, 2 * D_ATT)
    v = seg(2 * D_ATT, 3 * D_ATT)
    if pair_major:
        q_ref, kp_ref, vp_ref, kf_ref, vf_ref, z_ref, xbc_ref, dt_ref = out_refs
        put_pairs(q_ref, q)
        put_pairs(kp_ref, k)
        put_pairs(vp_ref, v)
    else:
        q_ref, kf_ref, vf_ref, z_ref, xbc_ref, dt_ref = out_refs
        q_ref[...] = q.astype(BF16)
    kf_ref[...] = k
    vf_ref[...] = v
    o = 3 * D_ATT
    z_ref[...] = seg(o, o + D_SSM).astype(BF16)
    o += D_SSM
    xbc_ref[...] = seg(o, o + D_CONV)
    o += D_CONV
    n_dt = w_ref.shape[0] - o
    w_dt = jnp.concatenate([w_ref[o:, :], jnp.zeros((DT_PAD - n_dt, D_MODEL), BF16)], axis=0)
    dt_ref[...] = _dot_nt(xb, w_dt)


def _in_proj(x, w_t, tm, keep, pair_major):
    b, t, _ = x.shape
    nt = t // tm
    skip = (t - keep) // tm
    row = lambda bi, i: (bi, i, 0)
    kept = lambda bi, i: (bi, jnp.maximum(i - skip, 0), 0)

    def spec(w, imap=row):
        return pl.BlockSpec((None, tm, w), imap)

    sds = jax.ShapeDtypeStruct
    pair_spec = pl.BlockSpec((None, N_PAIRS, tm, LANES), lambda bi, i: (bi, 0, i, 0))
    pair_shape = sds((b, N_PAIRS, t, LANES), F32)
    tail_specs = [spec(D_ATT, kept), spec(D_ATT, kept), spec(D_SSM), spec(D_CONV), spec(DT_PAD)]
    tail_shapes = [sds((b, keep, D_ATT), F32), sds((b, keep, D_ATT), F32),
                   sds((b, t, D_SSM), BF16), sds((b, t, D_CONV), F32), sds((b, t, DT_PAD), F32)]
    if pair_major:
        out_specs = [pair_spec] * 3 + tail_specs
        out_shape = [pair_shape] * 3 + tail_shapes
    else:
        out_specs = [spec(D_ATT)] + tail_specs
        out_shape = [sds((b, t, D_ATT), BF16)] + tail_shapes
    return pl.pallas_call(
        functools.partial(_in_proj_kernel, pair_major=pair_major),
        grid=(b, nt),
        in_specs=[spec(D_MODEL),
                  pl.BlockSpec(w_t.shape, lambda bi, i: (0, 0), pipeline_mode=pl.Buffered(1))],
        out_specs=out_specs,
        out_shape=out_shape,
        compiler_params=_params(("arbitrary", "arbitrary"), VMEM_LIMIT),
        name="in_proj",
    )(x, w_t)


def _branch_bias(dil):
    r = np.arange(WIN)[:, None]
    c = np.arange(2 * WIN)[None, :]
    ju = WIN + r - c
    valid = (ju >= 0) & (ju <= WIN)
    slopes = _alibi_slopes()[:, None, None]
    base = -slopes * (ju * dil)[None] * LOG2E
    full = np.where(valid[None], base, NEG)
    first = np.where((valid & (c >= WIN))[None], base, NEG)
    return np.stack([full, first]).astype(np.float32)


def _attn_block(qf, kblk, vblk, bias_ref, var, head0):
    q2 = jnp.concatenate([jnp.where(head0, qf, 0.0), jnp.where(head0, 0.0, qf)], axis=0).astype(BF16)
    s = _dot_nt(q2, kblk) + jnp.concatenate([bias_ref[var, 0], bias_ref[var, 1]], axis=0)
    m = jnp.max(s, axis=1, keepdims=True)
    p = jnp.exp2(s - m).astype(BF16)
    r0 = _dot(p[:WIN], vblk)
    r1 = _dot(p[WIN:], vblk)
    den = jnp.where(head0, r0[:, LANES:], r1[:, LANES:])
    num = jnp.where(head0, r0[:, :LANES], r1[:, :LANES])
    lse = jnp.where(head0, m[:WIN], m[WIN:]) + jnp.log2(den)
    return num / den, lse


def _attn_kernel(q_ref, k_ref, v_ref, b1_ref, b4_ref, b16_ref, o_ref,
                 q4_sc, k4_sc, v4_sc, kb1_sc, va1_sc, kb4_sc, va4_sc,
                 o1_sc, l1_sc, o16_sc, l16_sc, *, t):
    t4 = t // 4
    t16 = t // 16
    n1 = t // WIN
    n4 = t4 // WIN
    lane = lax.broadcasted_iota(jnp.int32, (1, LANES), 1)
    head0 = lane < HEAD_DIM
    zeros_k = jnp.zeros((WIN, LANES), BF16)
    zeros_v = jnp.zeros((WIN, 2 * LANES), BF16)

    def rows(start, n=WIN):
        return pl.ds(start, n)

    def with_ones(vf):
        return jnp.concatenate([vf.astype(BF16), jnp.ones(vf.shape, BF16)], axis=1)


    kb1_sc[0:WIN, :] = zeros_k
    va1_sc[0:WIN, :] = zeros_v
    for c in range(4):
        kb4_sc[c, 0:WIN, :] = zeros_k
        va4_sc[c, 0:WIN, :] = zeros_v

    for j in range(n1):
        src = rows(j * WIN)
        dst = rows(WIN + j * WIN)
        kb1_sc[dst, :] = k_ref[src, :].astype(BF16)
        va1_sc[dst, :] = with_ones(v_ref[src, :])

    for c in range(4):
        for j in range(n4):
            src = pl.ds(c + 4 * WIN * j, WIN, stride=4)
            dst = rows(c * t4 + j * WIN)
            kc = k_ref[src, :]
            vc = v_ref[src, :]
            k4_sc[dst, :] = kc
            v4_sc[dst, :] = vc
            q4_sc[dst, :] = q_ref[src, :]
            kb4_sc[c, rows(WIN + j * WIN), :] = kc.astype(BF16)
            va4_sc[c, rows(WIN + j * WIN), :] = with_ones(vc)

    for j in range(n1):
        out, lse = _attn_block(q_ref[rows(j * WIN), :], kb1_sc[rows(j * WIN, 2 * WIN), :],
                               va1_sc[rows(j * WIN, 2 * WIN), :], b1_ref, int(j == 0), head0)
        o1_sc[rows(j * WIN), :] = out
        l1_sc[rows(j * WIN), :] = lse

    for cls in range(16):
        r4, c = divmod(cls, 4)
        base = r4 * t4 + c
        src = pl.ds(base, t16, stride=4)
        kfull = jnp.concatenate([zeros_k, k4_sc[src, :].astype(BF16)], axis=0)
        vfull = jnp.concatenate([zeros_v, with_ones(v4_sc[src, :])], axis=0)
        for j in range(t16 // WIN):
            dst = pl.ds(base + 4 * WIN * j, WIN, stride=4)
            out, lse = _attn_block(q4_sc[dst, :], kfull[j * WIN:(j + 2) * WIN], vfull[j * WIN:(j + 2) * WIN],
                                   b16_ref, int(j == 0), head0)
            o16_sc[dst, :] = out
            l16_sc[dst, :] = lse

    for c in range(4):
        for j in range(n4):
            cm = rows(c * t4 + j * WIN)
            nat = pl.ds(c + 4 * WIN * j, WIN, stride=4)
            o4, l4 = _attn_block(q4_sc[cm, :], kb4_sc[c, rows(j * WIN, 2 * WIN), :],
                                 va4_sc[c, rows(j * WIN, 2 * WIN), :], b4_ref, int(j == 0), head0)
            l1 = l1_sc[nat, :]
            l16 = l16_sc[cm, :]
            m = jnp.maximum(jnp.maximum(l1, l4), l16)
            e1 = jnp.exp2(l1 - m)
            e4 = jnp.exp2(l4 - m)
            e16 = jnp.exp2(l16 - m)
            acc = e1 * o1_sc[nat, :] + e4 * o4 + e16 * o16_sc[cm, :]
            o_ref[nat, :] = acc / (e1 + e4 + e16)


def _attn(q, k, v):
    b, npairs, t, _ = q.shape
    assert t % (16 * WIN) == 0
    slab = pl.BlockSpec((None, None, t, LANES), lambda bi, hp: (bi, hp, 0, 0))
    biases = [jnp.asarray(_branch_bias(dil)) for _, dil in DILATED_BRANCHES]
    bias_spec = pl.BlockSpec((2, 2, WIN, 2 * WIN), lambda bi, hp: (0, hp, 0, 0))
    t4 = t // 4
    f32_slab = pltpu.VMEM((t, LANES), F32)
    scratch = [f32_slab, f32_slab, f32_slab,
               pltpu.VMEM((t + WIN, LANES), BF16), pltpu.VMEM((t + WIN, 2 * LANES), BF16),
               pltpu.VMEM((4, t4 + WIN, LANES), BF16), pltpu.VMEM((4, t4 + WIN, 2 * LANES), BF16)] + [f32_slab] * 4
    return pl.pallas_call(
        functools.partial(_attn_kernel, t=t),
        grid=(b, npairs),
        in_specs=[slab, slab, slab, bias_spec, bias_spec, bias_spec],
        out_specs=slab,
        out_shape=jax.ShapeDtypeStruct((b, npairs, t, LANES), F32),
        scratch_shapes=scratch,
        compiler_params=_params(("arbitrary", "arbitrary"), VMEM_LIMIT),
        name="dilated_attn",
    )(q, k, v, *biases)


def _sample_table(wb, tn, kpad):
    i = np.arange(tn)[:, None]
    c = np.arange(wb + kpad)[None, :]
    dist = wb + i - c
    mult = np.zeros(dist.shape)
    for window, dil in DILATED_BRANCHES:
        mult += (dist >= 0) & (dist <= window) & (dist % dil == 0) & (c < wb + tn)
    logm = np.where(mult > 0, np.log2(np.maximum(mult, 1)), NEG)
    slopes = _alibi_slopes()[:, None, None]
    tab = np.where(mult[None] > 0, -slopes * dist[None] * LOG2E + logm[None], NEG)
    return tab.reshape(N_HEADS * tn, wb + kpad).astype(np.float32)


def _sample_attn_heads(q_ref, kn_ref, vn_ref, kt_ref, vt_ref, tab_ref, o_ref, *, tn):
    nh, _, wb = kt_ref.shape
    kpad = tab_ref.shape[1] - wb
    width = nh * HEAD_DIM
    rows = nh * tn
    row_head = lax.shift_right_logical(lax.broadcasted_iota(jnp.int32, (rows, width), 0), int(math.log2(tn)))
    lane_head = lax.shift_right_logical(lax.broadcasted_iota(jnp.int32, (rows, width), 1), int(math.log2(HEAD_DIM)))
    diag = row_head == lane_head
    qt = jnp.concatenate([q_ref[...].astype(F32)] * nh, axis=0)
    qbd = jnp.where(diag, qt, 0.0).astype(BF16)
    pad = jnp.zeros((kpad - tn, width), F32)
    k_new = jnp.concatenate([kn_ref[...], pad], axis=0).astype(BF16)
    v_new = jnp.concatenate([vn_ref[...], pad], axis=0).astype(BF16)
    kt = kt_ref[...].reshape(width, wb).astype(BF16)
    vt = vt_ref[...].reshape(width, wb).astype(BF16)
    s = jnp.concatenate([_dot(qbd, kt), _dot_nt(qbd, k_new)], axis=1) + tab_ref[...]
    m = jnp.max(s, axis=1, keepdims=True)
    p = jnp.exp2(s - m).astype(BF16)
    den = jnp.sum(p.astype(F32), axis=1, keepdims=True)
    full = (_dot_nt(p[:, :wb], vt) + _dot(p[:, wb:], v_new)) / den
    full = jnp.where(diag, full, 0.0)
    acc = full[0:tn]
    for h in range(1, nh):
        acc = acc + full[h * tn:(h + 1) * tn]
    o_ref[...] = acc


def _expand_matrix(width):
    e = np.zeros((LANES, N_HEADS * width), np.float32)
    for h in range(N_HEADS):
        e[h, h * width:(h + 1) * width] = 1.0
    return e


def _ssd_kernel(*refs, l_in, n_chunks, group, rider_tn):
    (xbc_ref, dt_ref, z_ref, h0_ref, pre_ref, convw_ref, convb_ref, dtb_ref, a_ref, dskip_ref,
     gn_ref, tri_ref, e128_ref, e64_ref) = refs[:14]
    if rider_tn:
        rider_in = refs[14:20]
        y_ref, hout_ref, rider_out, xpad_sc, tail_sc, h_sc = refs[20:]
        _sample_attn_heads(*rider_in, rider_out, tn=rider_tn)
    else:
        y_ref, hout_ref, xpad_sc, tail_sc, h_sc = refs[14:]
    c = pl.program_id(1)

    @pl.when(c == 0)
    def _():
        h_sc[...] = h0_ref[...]
        for g in range(group):
            for s in range(D_CONV // LANES):
                tail_sc[g, s] = pre_ref[g, :, s * LANES:(s + 1) * LANES]

    for g in range(group):
        _ssd_chunk(xbc_ref.at[g], dt_ref.at[g], z_ref.at[g], convw_ref, convb_ref, dtb_ref, a_ref, dskip_ref,
                   gn_ref, tri_ref, e128_ref, e64_ref, y_ref.at[g], xpad_sc.at[g], tail_sc.at[g], h_sc.at[g],
                   l_in=l_in, n_chunks=n_chunks)

    @pl.when(c == n_chunks - 1)
    def _():
        hout_ref[...] = h_sc[...]


def _ssd_chunk(xbc_ref, dt_ref, z_ref, convw_ref, convb_ref, dtb_ref, a_ref, dskip_ref, gn_ref, tri_ref, e128_ref,
               e64_ref, y_ref, xpad_sc, tail_sc, h_sc, *, l_in, n_chunks):
    pad = CHUNK - l_in

    def time_rows(tau, n):
        return pl.ds(CONV_PITCH * (tau + CONV_PRE), n, stride=CONV_PITCH)

    def pad_rows(x):
        return x if pad == 0 else jnp.concatenate([x, jnp.zeros((pad, x.shape[1]), x.dtype)], axis=0)

    def conv_slab(s):
        lanes = slice(s * LANES, (s + 1) * LANES)
        xpad_sc[s, time_rows(-CONV_PRE, CONV_PRE), :] = tail_sc[s]
        xpad_sc[s, time_rows(0, l_in), :] = xbc_ref[:, lanes]
        acc = convb_ref[:, lanes]
        for w in range(CONV_WIDTH):
            acc = acc + convw_ref[w:w + 1, lanes] * xpad_sc[s, time_rows(w - (CONV_WIDTH - 1), l_in), :]
        if n_chunks > 1:
            tail_sc[s] = xbc_ref[CHUNK - CONV_PRE:, lanes]
        return _silu(acc)

    n_x = D_SSM // LANES
    n_bc = N_GROUPS * SSM_STATE // LANES

    draw = dt_ref[...] + dtb_ref[...]
    dt = jnp.maximum(draw, 0.0) + jnp.log1p(jnp.exp(-jnp.abs(draw)))
    la = pad_rows(dt * (a_ref[...] * LOG2E))
    tri = tri_ref[...]
    la_hi, la_lo = _split_bf16(la)
    cs = _dot(tri, la_hi) + _dot(tri, la_lo)
    cs_t = cs.T
    def expander(x, e_ref):
        hi_part, lo_part = _split_bf16(x)
        if pad == 0:
            return lambda lo, hi: _dot(hi_part, e_ref[:, lo:hi]) + _dot(lo_part, e_ref[:, lo:hi])
        full = _dot(hi_part, e_ref[...]) + _dot(lo_part, e_ref[...])
        return lambda lo, hi: full[:, lo:hi]

    cs128_of = expander(cs[:l_in], e128_ref)
    cs64_of = expander(cs[:l_in], e64_ref)
    dt64_of = expander(dt, e64_ref)

    lane = lax.broadcasted_iota(jnp.int32, (1, LANES), 1)
    head0 = lane < HEAD_DIM
    causal = (lax.broadcasted_iota(jnp.int32, (l_in, CHUNK), 0)
              >= lax.broadcasted_iota(jnp.int32, (l_in, CHUNK), 1))

    y_parts = []
    sumsq = None
    for g in range(N_GROUPS):
        xg = slice(g * GROUP_W, (g + 1) * GROUP_W)
        xs = jnp.concatenate([conv_slab(2 * g), conv_slab(2 * g + 1)], axis=1)
        b_g = pad_rows(conv_slab(n_x + g)).astype(BF16)
        c_g = conv_slab(n_x + n_bc + g).astype(BF16)
        cs128 = cs128_of(g * HEADS_PER_GROUP * LANES, (g + 1) * HEADS_PER_GROUP * LANES)
        cs64 = cs64_of(g * GROUP_W, (g + 1) * GROUP_W)
        xdt = xs * dt64_of(g * GROUP_W, (g + 1) * GROUP_W)
        xw = pad_rows(xdt * jnp.exp2(cs64[l_in - 1:l_in, :] - cs64))
        xdt_k = pad_rows(xdt)
        cb = _dot_nt(c_g, b_g)
        h_prev = h_sc[xg, :]
        y_g = _dot_nt(c_g, h_prev.astype(BF16)) * jnp.exp2(cs64)
        y_diag = []
        for hp in range(HEADS_PER_GROUP // 2):
            gmats = []
            for j in range(2):
                hl = 2 * hp + j
                seg = cs128[:, hl * LANES:(hl + 1) * LANES] - cs_t[g * HEADS_PER_GROUP + hl:g * HEADS_PER_GROUP + hl + 1, :]
                gmats.append((cb * jnp.exp2(jnp.where(causal, seg, NEG))).astype(BF16))
            xp = xdt_k[:, hp * LANES:(hp + 1) * LANES]
            rhs = jnp.concatenate([jnp.where(head0, xp, 0.0), jnp.where(head0, 0.0, xp)], axis=0).astype(BF16)
            y_diag.append(_dot(jnp.concatenate(gmats, axis=1), rhs))
        st = _dot(xw.T.astype(BF16), b_g)
        for j in range(HEADS_PER_GROUP):
            hs = slice(g * GROUP_W + j * HEAD_DIM, g * GROUP_W + (j + 1) * HEAD_DIM)
            dec = jnp.exp2(cs128[l_in - 1:l_in, j * LANES:(j + 1) * LANES])
            h_sc[hs, :] = h_sc[hs, :] * dec + st[j * HEAD_DIM:(j + 1) * HEAD_DIM, :]
        y_g = y_g + jnp.concatenate(y_diag, axis=1) + dskip_ref[:, xg] * xs
        y_g = y_g * _silu(z_ref[:, xg].astype(F32))
        part = jnp.sum(y_g * y_g, axis=-1, keepdims=True)
        sumsq = part if sumsq is None else sumsq + part
        y_parts.append(y_g)

    scale = lax.rsqrt(sumsq * (1.0 / D_SSM) + RMS_EPS)
    for g in range(N_GROUPS):
        xg = slice(g * GROUP_W, (g + 1) * GROUP_W)
        y_ref[:, xg] = (y_parts[g] * scale * gn_ref[:, xg]).astype(BF16)


def _ssd(xbc, dt, z, h0, pre, conv_w, conv_b, dtb_pad, a_pad, dskip64, gnorm, rider=None):
    b, t, _ = xbc.shape
    l_in = min(CHUNK, t)
    nc = t // l_in
    tri = jnp.asarray(np.tril(np.ones((CHUNK, CHUNK), np.float32)), BF16)
    e128 = jnp.asarray(_expand_matrix(LANES), BF16)
    e64 = jnp.asarray(_expand_matrix(HEAD_DIM), BF16)
    group = math.gcd(b, SSD_GROUP)
    step = lambda w: pl.BlockSpec((group, l_in, w), lambda bi, c: (bi, c, 0))
    per_b = lambda r, w: pl.BlockSpec((group, r, w), lambda bi, c: (bi, 0, 0))
    consts = [conv_w, conv_b, dtb_pad, a_pad, dskip64, gnorm, tri, e128, e64]
    args = [xbc, dt, z, h0, pre] + consts
    in_specs = ([step(D_CONV), step(DT_PAD), step(D_SSM), per_b(D_SSM, SSM_STATE), per_b(CONV_PRE, D_CONV)]
                + [_const_spec(a.shape) for a in consts])
    out_specs = [step(D_SSM), per_b(D_SSM, SSM_STATE)]
    out_shape = [jax.ShapeDtypeStruct((b, t, D_SSM), BF16), jax.ShapeDtypeStruct((b, D_SSM, SSM_STATE), F32)]
    rider_tn = 0
    if rider is not None:
        q, k_new, v_new, cache_kt, cache_vt = rider
        bs, rider_tn, _ = q.shape
        wb = cache_kt.shape[-1]
        steps = (b // group) * nc
        parts = steps // bs
        nh = N_HEADS // parts
        assert steps == parts * bs and nh * parts == N_HEADS and nh % 2 == 0 and rider_tn & (rider_tn - 1) == 0
        tab = jnp.asarray(_sample_table(wb, rider_tn, LANES))
        seq = lambda bi, c: (bi * nc + c) // parts
        part = lambda bi, c: (bi * nc + c) % parts
        new = pl.BlockSpec((None, rider_tn, nh * HEAD_DIM), lambda bi, c: (seq(bi, c), 0, part(bi, c)))
        cache = pl.BlockSpec((None, nh, HEAD_DIM, wb), lambda bi, c: (seq(bi, c), part(bi, c), 0, 0))
        args += [q, k_new, v_new, cache_kt, cache_vt, tab]
        in_specs += [new, new, new, cache, cache,
                     pl.BlockSpec((nh * rider_tn, tab.shape[1]), lambda bi, c: (part(bi, c), 0))]
        out_specs.append(new)
        out_shape.append(jax.ShapeDtypeStruct((bs, rider_tn, D_ATT), F32))
    return pl.pallas_call(
        functools.partial(_ssd_kernel, l_in=l_in, n_chunks=nc, group=group, rider_tn=rider_tn),
        grid=(b // group, nc),
        in_specs=in_specs,
        out_specs=out_specs,
        out_shape=out_shape,
        scratch_shapes=[pltpu.VMEM((group, D_CONV // LANES, CONV_PITCH * (CONV_PRE + CHUNK), LANES), F32),
                        pltpu.VMEM((group, D_CONV // LANES, CONV_PRE, LANES), F32),
                        pltpu.VMEM((group, D_SSM, SSM_STATE), F32)],
        compiler_params=_params(("arbitrary", "arbitrary"), VMEM_LIMIT),
        name="ssd",
    )(*args)


def _mix_ffn_kernel(a_ref, ys_ref, x_ref, ga_ref, wo_ref, g1_ref, b1_ref, wg_ref, wu_ref, wd_ref, g2_ref, b2_ref,
                    out_ref, *, pair_major, alpha):
    if pair_major:
        attn = jnp.concatenate([a_ref[hp] for hp in range(N_PAIRS)], axis=1)
    else:
        attn = a_ref[...]
    a = _rms_norm(attn, ga_ref[...]).astype(BF16)
    mix = _dot(a, wo_ref[0:D_ATT, :]) + _dot(ys_ref[...], wo_ref[D_ATT:, :])
    h = _layer_norm(alpha * x_ref[...] + mix, g1_ref[...], b1_ref[...])
    hb = h.astype(BF16)
    act = (_silu(_dot(hb, wg_ref[...])) * _dot(hb, wu_ref[...])).astype(BF16)
    out_ref[...] = _layer_norm(alpha * h + _dot(act, wd_ref[...]), g2_ref[...], b2_ref[...])


def _mix_ffn(attn, y_ssm, x, w, tm, alpha, pair_major):
    b, t, _ = x.shape
    row = lambda wd: pl.BlockSpec((None, tm, wd), lambda bi, i: (bi, i, 0))
    a_spec = pl.BlockSpec((None, N_PAIRS, tm, LANES), lambda bi, i: (bi, 0, i, 0)) if pair_major else row(D_ATT)
    single = lambda arr: pl.BlockSpec(arr.shape, lambda bi, i: (0, 0), pipeline_mode=pl.Buffered(1))
    vec = lambda arr: _const_spec(arr.shape)
    return pl.pallas_call(
        functools.partial(_mix_ffn_kernel, pair_major=pair_major, alpha=alpha),
        grid=(b, t // tm),
        in_specs=[a_spec, row(D_SSM), row(D_MODEL), vec(w["g_attn"]), single(w["w_out"]), vec(w["ln1_g"]),
                  vec(w["ln1_b"]), single(w["w_gate"]), single(w["w_up"]), single(w["w_down"]), vec(w["ln2_g"]),
                  vec(w["ln2_b"])],
        out_specs=row(D_MODEL),
        out_shape=jax.ShapeDtypeStruct((b, t, D_MODEL), F32),
        compiler_params=_params(("arbitrary", "arbitrary"), VMEM_LIMIT),
        name="mix_ffn",
    )(attn, y_ssm, x, w["g_attn"], w["w_out"], w["ln1_g"], w["ln1_b"], w["w_gate"], w["w_up"], w["w_down"],
      w["ln2_g"], w["ln2_b"])


def _pad_lanes(v, width=LANES):
    return jnp.pad(v.astype(F32), (0, width - v.shape[0])).reshape(1, width)


def _layer_weights(w_in, conv_w, conv_b, dt_bias, a_log, d_skip, attn_norm_g, ssm_norm_g, w_out, ln1_g, ln1_b,
                   w_gate, w_up, w_down, ln2_g, ln2_b):
    row = lambda v: v.astype(F32).reshape(1, -1)
    return dict(
        w_in=w_in.T.astype(BF16),
        conv_w=conv_w.astype(F32), conv_b=row(conv_b),
        dtb=_pad_lanes(dt_bias), a=_pad_lanes(-jnp.exp(a_log.astype(F32))),
        dskip=row(jnp.repeat(d_skip.astype(F32), HEAD_DIM)),
        g_attn=row(attn_norm_g), g_ssm=row(ssm_norm_g), w_out=w_out.astype(BF16),
        ln1_g=row(ln1_g), ln1_b=row(ln1_b), w_gate=w_gate.astype(BF16), w_up=w_up.astype(BF16),
        w_down=w_down.astype(BF16), ln2_g=row(ln2_g), ln2_b=row(ln2_b))


def _ssd_weights(w):
    return w["conv_w"], w["conv_b"], w["dtb"], w["a"], w["dskip"], w["g_ssm"]


def _layer(xp, xs, cache_k, cache_v, state_ssm, state_conv, w, alpha):
    bp, t, _ = xp.shape
    bs, tn, _ = xs.shape
    keep = min(MAX_WINDOW, t)
    rows = bs * tn
    flat = lambda a: a.reshape(1, rows, a.shape[-1])
    per_b = lambda a: a.reshape(bs, tn, a.shape[-1])
    heads = lambda a: a.reshape(a.shape[0], a.shape[1], N_HEADS, HEAD_DIM)
    state = lambda h: h.reshape(h.shape[0], N_HEADS, HEAD_DIM, SSM_STATE)

    q_s, kf_s, vf_s, z_s, xbc_s, dt_s = _in_proj(flat(xs), w["w_in"], rows, rows, False)
    kf_s, vf_s, xbc_s = per_b(kf_s), per_b(vf_s), per_b(xbc_s)
    q, kp, vp, kf, vf, z, xbc, dt = _in_proj(xp, w["w_in"], ROW_TILE, keep, True)
    attn = _attn(q, kp, vp)

    cache_t = lambda c: jnp.transpose(c, (0, 2, 3, 1))
    rider = (per_b(q_s), kf_s, vf_s, cache_t(cache_k), cache_t(cache_v))
    h0 = jnp.zeros((bp, D_SSM, SSM_STATE), F32)
    pre = jnp.zeros((bp, CONV_PRE, D_CONV), F32)
    y_ssm, h_fin, attn_s = _ssd(xbc, dt, z, h0, pre, *_ssd_weights(w), rider=rider)
    yp = _mix_ffn(attn, y_ssm, xp, w, ROW_TILE, alpha, True)
    conv_new = xbc[:, t - (CONV_WIDTH - 1):, :]

    pre_s = jnp.pad(state_conv.astype(F32), ((0, 0), (CONV_PRE - (CONV_WIDTH - 1), 0), (0, 0)))
    y_ssm_s, h_fin_s = _ssd(xbc_s, per_b(dt_s), per_b(z_s), state_ssm.reshape(bs, D_SSM, SSM_STATE).astype(F32),
                            pre_s, *_ssd_weights(w))
    ys = _mix_ffn(flat(attn_s), flat(y_ssm_s), flat(xs), w, rows, alpha, False)
    conv_new_s = jnp.concatenate([state_conv.astype(F32), xbc_s], axis=1)[:, -(CONV_WIDTH - 1):, :]
    return (yp, per_b(ys[0]), heads(kf), heads(vf), state(h_fin), conv_new,
            heads(kf_s), heads(vf_s), state(h_fin_s), conv_new_s)


def kernel(x_prompt, x_sample, cache_k_win, cache_v_win, state_ssm, state_conv, w_in, conv_w, conv_b, dt_bias, a_log, d_skip, attn_norm_g, ssm_norm_g, w_out, ln1_g, ln1_b, w_gate, w_up, w_down, ln2_g, ln2_b):
    depth = w_in.shape[0]
    alpha = (2.0 * depth) ** 0.25
    hp, hs = x_prompt, x_sample
    cols = [[] for _ in range(8)]
    for l in range(depth):
        w = _layer_weights(w_in[l], conv_w[l], conv_b[l], dt_bias[l], a_log[l], d_skip[l], attn_norm_g[l],
                           ssm_norm_g[l], w_out[l], ln1_g[l], ln1_b[l], w_gate[l], w_up[l], w_down[l],
                           ln2_g[l], ln2_b[l])
        hp, hs, *layer_state = _layer(hp, hs, cache_k_win[l], cache_v_win[l], state_ssm[l], state_conv[l], w, alpha)
        for dst, val in zip(cols, layer_state):
            dst.append(val)
    return (hp, hs) + tuple(jnp.stack(c) for c in cols)
```

```python
import functools
import math

import numpy as np
import jax
import jax.numpy as jnp
from jax import lax
from jax.experimental import pallas as pl
from jax.experimental.pallas import tpu as pltpu

F32 = jnp.float32
BF16 = jnp.bfloat16

D_MODEL = 1024
HEAD_DIM = 64
N_HEADS = 16
N_PAIRS = N_HEADS // 2
D_ATT = N_HEADS * HEAD_DIM
D_SSM = N_HEADS * HEAD_DIM
SSM_STATE = 128
N_GROUPS = 4
HEADS_PER_GROUP = N_HEADS // N_GROUPS
GROUP_W = HEADS_PER_GROUP * HEAD_DIM
CONV_WIDTH = 4
D_CONV = D_SSM + 2 * N_GROUPS * SSM_STATE
D_FF = 2816
CHUNK = 128
CONV_PRE = 8
CONV_PITCH = 2
SSD_GROUP = 2
DILATED_BRANCHES = ((128, 1), (512, 4), (2048, 16))
WIN = 128
MAX_WINDOW = 2048
LN_EPS = 1e-5
RMS_EPS = 1e-5
LANES = 128
DT_PAD = LANES
LOG2E = math.log2(math.e)
Q_SCALE = HEAD_DIM ** -0.5 * LOG2E
NEG = -1e30
VMEM_LIMIT = 56 * 1024 * 1024
ROW_TILE = 512


def _alibi_slopes():
    return 2.0 ** (-8.0 * np.arange(1, N_HEADS + 1) / N_HEADS)


def _params(sem, vmem=None):
    return pltpu.CompilerParams(dimension_semantics=sem, vmem_limit_bytes=vmem)


def _const_spec(shape):
    nd = len(shape)
    return pl.BlockSpec(shape, lambda *_: (0,) * nd)


def _split_bf16(x):
    hi = x.astype(BF16)
    lo = (x - hi.astype(F32)).astype(BF16)
    return hi, lo


def _dot(a, b):
    return jnp.dot(a, b, preferred_element_type=F32)


def _dot_nt(a, b):
    return lax.dot_general(a, b, (((1,), (1,)), ((), ())), preferred_element_type=F32)


def _dot_split(x, e):
    hi, lo = _split_bf16(x)
    return _dot(hi, e) + _dot(lo, e)


def _layer_norm(x, g, b):
    mu = jnp.mean(x, axis=-1, keepdims=True)
    xc = x - mu
    var = jnp.mean(xc * xc, axis=-1, keepdims=True)
    return xc * lax.rsqrt(var + LN_EPS) * g + b


def _rms_norm(x, g):
    return x * lax.rsqrt(jnp.mean(x * x, axis=-1, keepdims=True) + RMS_EPS) * g


def _silu(x):
    h = 0.5 * x
    return h + h * jnp.tanh(h)


def _in_proj_kernel(x_ref, w_ref, *out_refs, pair_major):
    xb = x_ref[...].astype(BF16)

    def seg(lo, hi):
        return _dot_nt(xb, w_ref[lo:hi, :])

    def put_pairs(ref, val):
        for hp in range(N_PAIRS):
            ref[hp] = val[:, hp * LANES:(hp + 1) * LANES]

    q = seg(0, D_ATT) * Q_SCALE
    k = seg(D_ATT, 2 * D_ATT)
    v = seg(2 * D_ATT, 3 * D_ATT)
    if pair_major:
        q_ref, kp_ref, vp_ref, kf_ref, vf_ref, z_ref, xbc_ref, dt_ref = out_refs
        put_pairs(q_ref, q)
        put_pairs(kp_ref, k)
        put_pairs(vp_ref, v)
    else:
        q_ref, kf_ref, vf_ref, z_ref, xbc_ref, dt_ref = out_refs
        q_ref[...] = q.astype(BF16)
    kf_ref[...] = k
    vf_ref[...] = v
    o = 3 * D_ATT
    z_ref[...] = seg(o, o + D_SSM).astype(BF16)
    o += D_SSM
    xbc_ref[...] = seg(o, o + D_CONV)
    o += D_CONV
    n_dt = w_ref.shape[0] - o
    w_dt = jnp.concatenate([w_ref[o:, :], jnp.zeros((DT_PAD - n_dt, D_MODEL), BF16)], axis=0)
    dt_ref[...] = _dot_nt(xb, w_dt)


def _in_proj(x, w_t, tm, keep, pair_major):
    b, t, _ = x.shape
    nt = t // tm
    skip = (t - keep) // tm
    row = lambda bi, i: (bi, i, 0)
    kept = lambda bi, i: (bi, jnp.maximum(i - skip, 0), 0)

    def spec(w, imap=row):
        return pl.BlockSpec((None, tm, w), imap)

    sds = jax.ShapeDtypeStruct
    pair_spec = pl.BlockSpec((None, N_PAIRS, tm, LANES), lambda bi, i: (bi, 0, i, 0))
    pair_shape = sds((b, N_PAIRS, t, LANES), F32)
    tail_specs = [spec(D_ATT, kept), spec(D_ATT, kept), spec(D_SSM), spec(D_CONV), spec(DT_PAD)]
    tail_shapes = [sds((b, keep, D_ATT), F32), sds((b, keep, D_ATT), F32),
                   sds((b, t, D_SSM), BF16), sds((b, t, D_CONV), F32), sds((b, t, DT_PAD), F32)]
    if pair_major:
        out_specs = [pair_spec] * 3 + tail_specs
        out_shape = [pair_shape] * 3 + tail_shapes
    else:
        out_specs = [spec(D_ATT)] + tail_specs
        out_shape = [sds((b, t, D_ATT), BF16)] + tail_shapes
    return pl.pallas_call(
        functools.partial(_in_proj_kernel, pair_major=pair_major),
        grid=(b, nt),
        in_specs=[spec(D_MODEL),
                  pl.BlockSpec(w_t.shape, lambda bi, i: (0, 0), pipeline_mode=pl.Buffered(1))],
        out_specs=out_specs,
        out_shape=out_shape,
        compiler_params=_params(("arbitrary", "arbitrary"), VMEM_LIMIT),
        name="in_proj",
    )(x, w_t)


def _branch_bias(dil):
    r = np.arange(WIN)[:, None]
    c = np.arange(2 * WIN)[None, :]
    ju = WIN + r - c
    valid = (ju >= 0) & (ju <= WIN)
    slopes = _alibi_slopes()[:, None, None]
    base = -slopes * (ju * dil)[None] * LOG2E
    full = np.where(valid[None], base, NEG)
    first = np.where((valid & (c >= WIN))[None], base, NEG)
    return np.stack([full, first]).astype(np.float32)


def _attn_block(qf, kblk, vblk, bias_ref, var, head0):
    q2 = jnp.concatenate([jnp.where(head0, qf, 0.0), jnp.where(head0, 0.0, qf)], axis=0).astype(BF16)
    s = _dot_nt(q2, kblk) + jnp.concatenate([bias_ref[var, 0], bias_ref[var, 1]], axis=0)
    m = jnp.max(s, axis=1, keepdims=True)
    p = jnp.exp2(s - m).astype(BF16)
    r0 = _dot(p[:WIN], vblk)
    r1 = _dot(p[WIN:], vblk)
    den = jnp.where(head0, r0[:, LANES:], r1[:, LANES:])
    num = jnp.where(head0, r0[:, :LANES], r1[:, :LANES])
    lse = jnp.where(head0, m[:WIN], m[WIN:]) + jnp.log2(den)
    return num / den, lse


def _attn_kernel(q_ref, k_ref, v_ref, b1_ref, b4_ref, b16_ref, o_ref,
                 q4_sc, k4_sc, v4_sc, kb1_sc, va1_sc, kb4_sc, va4_sc,
                 o1_sc, l1_sc, o16_sc, l16_sc, *, t):
    t4 = t // 4
    t16 = t // 16
    n1 = t // WIN
    n4 = t4 // WIN
    lane = lax.broadcasted_iota(jnp.int32, (1, LANES), 1)
    head0 = lane < HEAD_DIM
    zeros_k = jnp.zeros((WIN, LANES), BF16)
    zeros_v = jnp.zeros((WIN, 2 * LANES), BF16)

    def rows(start, n=WIN):
        return pl.ds(start, n)

    def with_ones(vf):
        return jnp.concatenate([vf.astype(BF16), jnp.ones(vf.shape, BF16)], axis=1)


    kb1_sc[0:WIN, :] = zeros_k
    va1_sc[0:WIN, :] = zeros_v
    for c in range(4):
        kb4_sc[c, 0:WIN, :] = zeros_k
        va4_sc[c, 0:WIN, :] = zeros_v

    for j in range(n1):
        src = rows(j * WIN)
        dst = rows(WIN + j * WIN)
        kb1_sc[dst, :] = k_ref[src, :].astype(BF16)
        va1_sc[dst, :] = with_ones(v_ref[src, :])

    for c in range(4):
        for j in range(n4):
            src = pl.ds(c + 4 * WIN * j, WIN, stride=4)
            dst = rows(c * t4 + j * WIN)
            kc = k_ref[src, :]
            vc = v_ref[src, :]
            k4_sc[dst, :] = kc
            v4_sc[dst, :] = vc
            q4_sc[dst, :] = q_ref[src, :]
            kb4_sc[c, rows(WIN + j * WIN), :] = kc.astype(BF16)
            va4_sc[c, rows(WIN + j * WIN), :] = with_ones(vc)

    for j in range(n1):
        out, lse = _attn_block(q_ref[rows(j * WIN), :], kb1_sc[rows(j * WIN, 2 * WIN), :],
                               va1_sc[rows(j * WIN, 2 * WIN), :], b1_ref, int(j == 0), head0)
        o1_sc[rows(j * WIN), :] = out
        l1_sc[rows(j * WIN), :] = lse

    for cls in range(16):
        r4, c = divmod(cls, 4)
        base = r4 * t4 + c
        src = pl.ds(base, t16, stride=4)
        kfull = jnp.concatenate([zeros_k, k4_sc[src, :].astype(BF16)], axis=0)
        vfull = jnp.concatenate([zeros_v, with_ones(v4_sc[src, :])], axis=0)
        for j in range(t16 // WIN):
            dst = pl.ds(base + 4 * WIN * j, WIN, stride=4)
            out, lse = _attn_block(q4_sc[dst, :], kfull[j * WIN:(j + 2) * WIN], vfull[j * WIN:(j + 2) * WIN],
                                   b16_ref, int(j == 0), head0)
            o16_sc[dst, :] = out
            l16_sc[dst, :] = lse

    for c in range(4):
        for j in range(n4):
            cm = rows(c * t4 + j * WIN)
            nat = pl.ds(c + 4 * WIN * j, WIN, stride=4)
            o4, l4 = _attn_block(q4_sc[cm, :], kb4_sc[c, rows(j * WIN, 2 * WIN), :],
                                 va4_sc[c, rows(j * WIN, 2 * WIN), :], b4_ref, int(j == 0), head0)
            l1 = l1_sc[nat, :]
            l16 = l16_sc[cm, :]
            m = jnp.maximum(jnp.maximum(l1, l4), l16)
            e1 = jnp.exp2(l1 - m)
            e4 = jnp.exp2(l4 - m)
            e16 = jnp.exp2(l16 - m)
            acc = e1 * o1_sc[nat, :] + e4 * o4 + e16 * o16_sc[cm, :]
            o_ref[nat, :] = acc / (e1 + e4 + e16)


def _attn(q, k, v):
    b, npairs, t, _ = q.shape
    assert t % (16 * WIN) == 0
    slab = pl.BlockSpec((None, None, t, LANES), lambda bi, hp: (bi, hp, 0, 0))
    biases = [jnp.asarray(_branch_bias(dil)) for _, dil in DILATED_BRANCHES]
    bias_spec = pl.BlockSpec((2, 2, WIN, 2 * WIN), lambda bi, hp: (0, hp, 0, 0))
    t4 = t // 4
    f32_slab = pltpu.VMEM((t, LANES), F32)
    scratch = [f32_slab, f32_slab, f32_slab,
               pltpu.VMEM((t + WIN, LANES), BF16), pltpu.VMEM((t + WIN, 2 * LANES), BF16),
               pltpu.VMEM((4, t4 + WIN, LANES), BF16), pltpu.VMEM((4, t4 + WIN, 2 * LANES), BF16)] + [f32_slab] * 4
    return pl.pallas_call(
        functools.partial(_attn_kernel, t=t),
        grid=(b, npairs),
        in_specs=[slab, slab, slab, bias_spec, bias_spec, bias_spec],
        out_specs=slab,
        out_shape=jax.ShapeDtypeStruct((b, npairs, t, LANES), F32),
        scratch_shapes=scratch,
        compiler_params=_params(("arbitrary", "arbitrary"), VMEM_LIMIT),
        name="dilated_attn",
    )(q, k, v, *biases)


def _sample_table(wb, tn, kpad):
    i = np.arange(tn)[:, None]
    c = np.arange(wb + kpad)[None, :]
    dist = wb + i - c
    mult = np.zeros(dist.shape)
    for window, dil in DILATED_BRANCHES:
        mult += (dist >= 0) & (dist <= window) & (dist % dil == 0) & (c < wb + tn)
    logm = np.where(mult > 0, np.log2(np.maximum(mult, 1)), NEG)
    slopes = _alibi_slopes()[:, None, None]
    tab = np.where(mult[None] > 0, -slopes * dist[None] * LOG2E + logm[None], NEG)
    return tab.reshape(N_HEADS * tn, wb + kpad).astype(np.float32)


def _sample_attn_heads(q_ref, kn_ref, vn_ref, kt_ref, vt_ref, tab_ref, o_ref, *, tn):
    nh, _, wb = kt_ref.shape
    kpad = tab_ref.shape[1] - wb
    width = nh * HEAD_DIM
    rows = nh * tn
    row_head = lax.shift_right_logical(lax.broadcasted_iota(jnp.int32, (rows, width), 0), int(math.log2(tn)))
    lane_head = lax.shift_right_logical(lax.broadcasted_iota(jnp.int32, (rows, width), 1), int(math.log2(HEAD_DIM)))
    diag = row_head == lane_head
    qt = jnp.concatenate([q_ref[...].astype(F32)] * nh, axis=0)
    qbd = jnp.where(diag, qt, 0.0).astype(BF16)
    pad = jnp.zeros((kpad - tn, width), F32)
    k_new = jnp.concatenate([kn_ref[...], pad], axis=0).astype(BF16)
    v_new = jnp.concatenate([vn_ref[...], pad], axis=0).astype(BF16)
    n_c = 4
    cw = wb // n_c
    chunk = lambda ref, c: ref[:, :, c * cw:(c + 1) * cw].reshape(width, cw).astype(BF16)
    s = jnp.concatenate([_dot(qbd, chunk(kt_ref, c)) for c in range(n_c)] + [_dot_nt(qbd, k_new)], axis=1)
    s = s + tab_ref[...]
    m = jnp.max(s, axis=1, keepdims=True)
    p = jnp.exp2(s - m).astype(BF16)
    den = jnp.sum(p.astype(F32), axis=1, keepdims=True)
    full = _dot(p[:, wb:], v_new)
    for c in range(n_c):
        full = full + _dot_nt(p[:, c * cw:(c + 1) * cw], chunk(vt_ref, c))
    full = full / den
    full = jnp.where(diag, full, 0.0)
    acc = full[0:tn]
    for h in range(1, nh):
        acc = acc + full[h * tn:(h + 1) * tn]
    o_ref[...] = acc


def _expand_matrix(width):
    e = np.zeros((LANES, N_HEADS * width), np.float32)
    for h in range(N_HEADS):
        e[h, h * width:(h + 1) * width] = 1.0
    return e


def _ssd_kernel(*refs, l_in, n_chunks, group, rider_tn):
    (xbc_ref, dt_ref, z_ref, h0_ref, pre_ref, convw_ref, convb_ref, dtb_ref, a_ref, dskip_ref,
     gn_ref, tri_ref, e128_ref, e64_ref) = refs[:14]
    if rider_tn:
        rider_in = refs[14:20]
        y_ref, hout_ref, rider_out, xpad_sc, tail_sc, h_sc = refs[20:]
        _sample_attn_heads(*rider_in, rider_out, tn=rider_tn)
    else:
        y_ref, hout_ref, xpad_sc, tail_sc, h_sc = refs[14:]
    c = pl.program_id(1)

    @pl.when(c == 0)
    def _():
        h_sc[...] = h0_ref[...]
        for g in range(group):
            for s in range(D_CONV // LANES):
                tail_sc[g, s] = pre_ref[g, :, s * LANES:(s + 1) * LANES]

    for g in range(group):
        _ssd_chunk(xbc_ref.at[g], dt_ref.at[g], z_ref.at[g], convw_ref, convb_ref, dtb_ref, a_ref, dskip_ref,
                   gn_ref, tri_ref, e128_ref, e64_ref, y_ref.at[g], xpad_sc.at[g], tail_sc.at[g], h_sc.at[g],
                   l_in=l_in, n_chunks=n_chunks)

    @pl.when(c == n_chunks - 1)
    def _():
        hout_ref[...] = h_sc[...]


def _ssd_chunk(xbc_ref, dt_ref, z_ref, convw_ref, convb_ref, dtb_ref, a_ref, dskip_ref, gn_ref, tri_ref, e128_ref,
               e64_ref, y_ref, xpad_sc, tail_sc, h_sc, *, l_in, n_chunks):
    pad = CHUNK - l_in

    def time_rows(tau, n):
        return pl.ds(CONV_PITCH * (tau + CONV_PRE), n, stride=CONV_PITCH)

    def pad_rows(x):
        return x if pad == 0 else jnp.concatenate([x, jnp.zeros((pad, x.shape[1]), x.dtype)], axis=0)

    def conv_slab(s):
        lanes = slice(s * LANES, (s + 1) * LANES)
        xpad_sc[s, time_rows(-CONV_PRE, CONV_PRE), :] = tail_sc[s]
        xpad_sc[s, time_rows(0, l_in), :] = xbc_ref[:, lanes]
        acc = convb_ref[:, lanes]
        for w in range(CONV_WIDTH):
            acc = acc + convw_ref[w:w + 1, lanes] * xpad_sc[s, time_rows(w - (CONV_WIDTH - 1), l_in), :]
        if n_chunks > 1:
            tail_sc[s] = xbc_ref[CHUNK - CONV_PRE:, lanes]
        return _silu(acc)

    n_x = D_SSM // LANES
    n_bc = N_GROUPS * SSM_STATE // LANES

    draw = dt_ref[...] + dtb_ref[...]
    dt = jnp.maximum(draw, 0.0) + jnp.log1p(jnp.exp(-jnp.abs(draw)))
    la = pad_rows(dt * (a_ref[...] * LOG2E))
    tri = tri_ref[...]
    la_hi, la_lo = _split_bf16(la)
    cs = _dot(tri, la_hi) + _dot(tri, la_lo)
    cs_t = cs.T
    def expander(x, e_ref):
        hi_part, lo_part = _split_bf16(x)
        if pad == 0:
            return lambda lo, hi: _dot(hi_part, e_ref[:, lo:hi]) + _dot(lo_part, e_ref[:, lo:hi])
        full = _dot(hi_part, e_ref[...]) + _dot(lo_part, e_ref[...])
        return lambda lo, hi: full[:, lo:hi]

    cs128_of = expander(cs[:l_in], e128_ref)
    cs64_of = expander(cs[:l_in], e64_ref)
    dt64_of = expander(dt, e64_ref)

    lane = lax.broadcasted_iota(jnp.int32, (1, LANES), 1)
    head0 = lane < HEAD_DIM
    causal = (lax.broadcasted_iota(jnp.int32, (l_in, CHUNK), 0)
              >= lax.broadcasted_iota(jnp.int32, (l_in, CHUNK), 1))

    y_parts = []
    sumsq = None
    for g in range(N_GROUPS):
        xg = slice(g * GROUP_W, (g + 1) * GROUP_W)
        xs = jnp.concatenate([conv_slab(2 * g), conv_slab(2 * g + 1)], axis=1)
        b_g = pad_rows(conv_slab(n_x + g)).astype(BF16)
        c_g = conv_slab(n_x + n_bc + g).astype(BF16)
        cs128 = cs128_of(g * HEADS_PER_GROUP * LANES, (g + 1) * HEADS_PER_GROUP * LANES)
        cs64 = cs64_of(g * GROUP_W, (g + 1) * GROUP_W)
        xdt = xs * dt64_of(g * GROUP_W, (g + 1) * GROUP_W)
        xw = pad_rows(xdt * jnp.exp2(cs64[l_in - 1:l_in, :] - cs64))
        xdt_k = pad_rows(xdt)
        cb = _dot_nt(c_g, b_g)
        h_prev = h_sc[xg, :]
        y_g = _dot_nt(c_g, h_prev.astype(BF16)) * jnp.exp2(cs64)
        y_diag = []
        for hp in range(HEADS_PER_GROUP // 2):
            gmats = []
            for j in range(2):
                hl = 2 * hp + j
                seg = cs128[:, hl * LANES:(hl + 1) * LANES] - cs_t[g * HEADS_PER_GROUP + hl:g * HEADS_PER_GROUP + hl + 1, :]
                gmats.append((cb * jnp.exp2(jnp.where(causal, seg, NEG))).astype(BF16))
            xp = xdt_k[:, hp * LANES:(hp + 1) * LANES]
            rhs = jnp.concatenate([jnp.where(head0, xp, 0.0), jnp.where(head0, 0.0, xp)], axis=0).astype(BF16)
            y_diag.append(_dot(jnp.concatenate(gmats, axis=1), rhs))
        st = _dot(xw.T.astype(BF16), b_g)
        for j in range(HEADS_PER_GROUP):
            hs = slice(g * GROUP_W + j * HEAD_DIM, g * GROUP_W + (j + 1) * HEAD_DIM)
            dec = jnp.exp2(cs128[l_in - 1:l_in, j * LANES:(j + 1) * LANES])
            h_sc[hs, :] = h_sc[hs, :] * dec + st[j * HEAD_DIM:(j + 1) * HEAD_DIM, :]
        y_g = y_g + jnp.concatenate(y_diag, axis=1) + dskip_ref[:, xg] * xs
        y_g = y_g * _silu(z_ref[:, xg].astype(F32))
        part = jnp.sum(y_g * y_g, axis=-1, keepdims=True)
        sumsq = part if sumsq is None else sumsq + part
        y_parts.append(y_g)

    scale = lax.rsqrt(sumsq * (1.0 / D_SSM) + RMS_EPS)
    for g in range(N_GROUPS):
        xg = slice(g * GROUP_W, (g + 1) * GROUP_W)
        y_ref[:, xg] = (y_parts[g] * scale * gn_ref[:, xg]).astype(BF16)


def _ssd(xbc, dt, z, h0, pre, conv_w, conv_b, dtb_pad, a_pad, dskip64, gnorm, rider=None):
    b, t, _ = xbc.shape
    l_in = min(CHUNK, t)
    nc = t // l_in
    tri = jnp.asarray(np.tril(np.ones((CHUNK, CHUNK), np.float32)), BF16)
    e128 = jnp.asarray(_expand_matrix(LANES), BF16)
    e64 = jnp.asarray(_expand_matrix(HEAD_DIM), BF16)
    group = math.gcd(b, SSD_GROUP)
    step = lambda w: pl.BlockSpec((group, l_in, w), lambda bi, c: (bi, c, 0))
    per_b = lambda r, w: pl.BlockSpec((group, r, w), lambda bi, c: (bi, 0, 0))
    consts = [conv_w, conv_b, dtb_pad, a_pad, dskip64, gnorm, tri, e128, e64]
    args = [xbc, dt, z, h0, pre] + consts
    in_specs = ([step(D_CONV), step(DT_PAD), step(D_SSM), per_b(D_SSM, SSM_STATE), per_b(CONV_PRE, D_CONV)]
                + [_const_spec(a.shape) for a in consts])
    out_specs = [step(D_SSM), per_b(D_SSM, SSM_STATE)]
    out_shape = [jax.ShapeDtypeStruct((b, t, D_SSM), BF16), jax.ShapeDtypeStruct((b, D_SSM, SSM_STATE), F32)]
    rider_tn = 0
    if rider is not None:
        q, k_new, v_new, cache_kt, cache_vt = rider
        bs, rider_tn, _ = q.shape
        wb = cache_kt.shape[-1]
        steps = (b // group) * nc
        parts = steps // bs
        nh = N_HEADS // parts
        assert steps == parts * bs and nh * parts == N_HEADS and nh % 2 == 0 and rider_tn & (rider_tn - 1) == 0
        tab = jnp.asarray(_sample_table(wb, rider_tn, LANES))
        seq = lambda bi, c: (bi * nc + c) // parts
        part = lambda bi, c: (bi * nc + c) % parts
        new = pl.BlockSpec((None, rider_tn, nh * HEAD_DIM), lambda bi, c: (seq(bi, c), 0, part(bi, c)))
        cache = pl.BlockSpec((None, nh, HEAD_DIM, wb), lambda bi, c: (seq(bi, c), part(bi, c), 0, 0))
        args += [q, k_new, v_new, cache_kt, cache_vt, tab]
        in_specs += [new, new, new, cache, cache,
                     pl.BlockSpec((nh * rider_tn, tab.shape[1]), lambda bi, c: (part(bi, c), 0))]
        out_specs.append(new)
        out_shape.append(jax.ShapeDtypeStruct((bs, rider_tn, D_ATT), F32))
    return pl.pallas_call(
        functools.partial(_ssd_kernel, l_in=l_in, n_chunks=nc, group=group, rider_tn=rider_tn),
        grid=(b // group, nc),
        in_specs=in_specs,
        out_specs=out_specs,
        out_shape=out_shape,
        scratch_shapes=[pltpu.VMEM((group, D_CONV // LANES, CONV_PITCH * (CONV_PRE + CHUNK), LANES), F32),
                        pltpu.VMEM((group, D_CONV // LANES, CONV_PRE, LANES), F32),
                        pltpu.VMEM((group, D_SSM, SSM_STATE), F32)],
        compiler_params=_params(("arbitrary", "arbitrary"), VMEM_LIMIT),
        name="ssd",
    )(*args)


def _mix_ffn_kernel(a_ref, ys_ref, x_ref, ga_ref, wo_ref, g1_ref, b1_ref, wg_ref, wu_ref, wd_ref, g2_ref, b2_ref,
                    out_ref, *, pair_major, alpha):
    if pair_major:
        attn = jnp.concatenate([a_ref[hp] for hp in range(N_PAIRS)], axis=1)
    else:
        attn = a_ref[...]
    a = _rms_norm(attn, ga_ref[...]).astype(BF16)
    mix = _dot(a, wo_ref[0:D_ATT, :]) + _dot(ys_ref[...], wo_ref[D_ATT:, :])
    h = _layer_norm(alpha * x_ref[...] + mix, g1_ref[...], b1_ref[...])
    hb = h.astype(BF16)
    act = (_silu(_dot(hb, wg_ref[...])) * _dot(hb, wu_ref[...])).astype(BF16)
    out_ref[...] = _layer_norm(alpha * h + _dot(act, wd_ref[...]), g2_ref[...], b2_ref[...])


def _mix_ffn(attn, y_ssm, x, w, tm, alpha, pair_major):
    b, t, _ = x.shape
    row = lambda wd: pl.BlockSpec((None, tm, wd), lambda bi, i: (bi, i, 0))
    a_spec = pl.BlockSpec((None, N_PAIRS, tm, LANES), lambda bi, i: (bi, 0, i, 0)) if pair_major else row(D_ATT)
    single = lambda arr: pl.BlockSpec(arr.shape, lambda bi, i: (0, 0), pipeline_mode=pl.Buffered(1))
    vec = lambda arr: _const_spec(arr.shape)
    return pl.pallas_call(
        functools.partial(_mix_ffn_kernel, pair_major=pair_major, alpha=alpha),
        grid=(b, t // tm),
        in_specs=[a_spec, row(D_SSM), row(D_MODEL), vec(w["g_attn"]), single(w["w_out"]), vec(w["ln1_g"]),
                  vec(w["ln1_b"]), single(w["w_gate"]), single(w["w_up"]), single(w["w_down"]), vec(w["ln2_g"]),
                  vec(w["ln2_b"])],
        out_specs=row(D_MODEL),
        out_shape=jax.ShapeDtypeStruct((b, t, D_MODEL), F32),
        compiler_params=_params(("arbitrary", "arbitrary"), VMEM_LIMIT),
        name="mix_ffn",
    )(attn, y_ssm, x, w["g_attn"], w["w_out"], w["ln1_g"], w["ln1_b"], w["w_gate"], w["w_up"], w["w_down"],
      w["ln2_g"], w["ln2_b"])


def _pad_lanes(v, width=LANES):
    return jnp.pad(v.astype(F32), (0, width - v.shape[0])).reshape(1, width)


def _layer_weights(w_in, conv_w, conv_b, dt_bias, a_log, d_skip, attn_norm_g, ssm_norm_g, w_out, ln1_g, ln1_b,
                   w_gate, w_up, w_down, ln2_g, ln2_b):
    row = lambda v: v.astype(F32).reshape(1, -1)
    return dict(
        w_in=w_in.T.astype(BF16),
        conv_w=conv_w.astype(F32), conv_b=row(conv_b),
        dtb=_pad_lanes(dt_bias), a=_pad_lanes(-jnp.exp(a_log.astype(F32))),
        dskip=row(jnp.repeat(d_skip.astype(F32), HEAD_DIM)),
        g_attn=row(attn_norm_g), g_ssm=row(ssm_norm_g), w_out=w_out.astype(BF16),
        ln1_g=row(ln1_g), ln1_b=row(ln1_b), w_gate=w_gate.astype(BF16), w_up=w_up.astype(BF16),
        w_down=w_down.astype(BF16), ln2_g=row(ln2_g), ln2_b=row(ln2_b))


def _ssd_weights(w):
    return w["conv_w"], w["conv_b"], w["dtb"], w["a"], w["dskip"], w["g_ssm"]


def _layer(xp, xs, cache_k, cache_v, state_ssm, state_conv, w, alpha):
    bp, t, _ = xp.shape
    bs, tn, _ = xs.shape
    keep = min(MAX_WINDOW, t)
    rows = bs * tn
    flat = lambda a: a.reshape(1, rows, a.shape[-1])
    per_b = lambda a: a.reshape(bs, tn, a.shape[-1])
    heads = lambda a: a.reshape(a.shape[0], a.shape[1], N_HEADS, HEAD_DIM)
    state = lambda h: h.reshape(h.shape[0], N_HEADS, HEAD_DIM, SSM_STATE)

    q_s, kf_s, vf_s, z_s, xbc_s, dt_s = _in_proj(flat(xs), w["w_in"], rows, rows, False)
    kf_s, vf_s, xbc_s = per_b(kf_s), per_b(vf_s), per_b(xbc_s)
    q, kp, vp, kf, vf, z, xbc, dt = _in_proj(xp, w["w_in"], ROW_TILE, keep, True)
    attn = _attn(q, kp, vp)

    cache_t = lambda c: jnp.transpose(c, (0, 2, 3, 1))
    rider = (per_b(q_s), kf_s, vf_s, cache_t(cache_k), cache_t(cache_v))
    h0 = jnp.zeros((bp, D_SSM, SSM_STATE), F32)
    pre = jnp.zeros((bp, CONV_PRE, D_CONV), F32)
    y_ssm, h_fin, attn_s = _ssd(xbc, dt, z, h0, pre, *_ssd_weights(w), rider=rider)
    yp = _mix_ffn(attn, y_ssm, xp, w, ROW_TILE, alpha, True)
    conv_new = xbc[:, t - (CONV_WIDTH - 1):, :]

    pre_s = jnp.pad(state_conv.astype(F32), ((0, 0), (CONV_PRE - (CONV_WIDTH - 1), 0), (0, 0)))
    y_ssm_s, h_fin_s = _ssd(xbc_s, per_b(dt_s), per_b(z_s), state_ssm.reshape(bs, D_SSM, SSM_STATE).astype(F32),
                            pre_s, *_ssd_weights(w))
    ys = _mix_ffn(flat(attn_s), flat(y_ssm_s), flat(xs), w, rows, alpha, False)
    conv_new_s = jnp.concatenate([state_conv.astype(F32), xbc_s], axis=1)[:, -(CONV_WIDTH - 1):, :]
    return (yp, per_b(ys[0]), heads(kf), heads(vf), state(h_fin), conv_new,
            heads(kf_s), heads(vf_s), state(h_fin_s), conv_new_s)


def kernel(x_prompt, x_sample, cache_k_win, cache_v_win, state_ssm, state_conv, w_in, conv_w, conv_b, dt_bias, a_log, d_skip, attn_norm_g, ssm_norm_g, w_out, ln1_g, ln1_b, w_gate, w_up, w_down, ln2_g, ln2_b):
    depth = w_in.shape[0]
    alpha = (2.0 * depth) ** 0.25
    hp, hs = x_prompt, x_sample
    cols = [[] for _ in range(8)]
    for l in range(depth):
        w = _layer_weights(w_in[l], conv_w[l], conv_b[l], dt_bias[l], a_log[l], d_skip[l], attn_norm_g[l],
                           ssm_norm_g[l], w_out[l], ln1_g[l], ln1_b[l], w_gate[l], w_up[l], w_down[l],
                           ln2_g[l], ln2_b[l])
        hp, hs, *layer_state = _layer(hp, hs, cache_k_win[l], cache_v_win[l], state_ssm[l], state_conv[l], w, alpha)
        for dst, val in zip(cols, layer_state):
            dst.append(val)
    return (hp, hs) + tuple(jnp.stack(c) for c in cols)
```
